```python
import jax, jax.numpy as jnp
from jax import lax
import numpy as np

D_MODEL = 1024
BATCH = 8
SEQ = 8192
DEPTH = 1

N_META = 16
RET_HEADS = 8
RET_QK_DIM = 128
RET_V_DIM = 128
RET_CHUNK = 128
ROPE_BASE = 10000.0
LRU_WIDTH = D_MODEL
LRU_BLOCKS = 4
LRU_BLOCK = LRU_WIDTH // LRU_BLOCKS
CONV_WIDTH = 4
LRU_C = 8.0
FFN_HIDDEN = ((8 * D_MODEL // 3 + 255) // 256) * 256
RET_QK = RET_HEADS * RET_QK_DIM
RET_V = RET_HEADS * RET_V_DIM
IN_COLS = 2 * RET_QK + 2 * RET_V + 2 * LRU_WIDTH + 2 * D_MODEL
NORM_EPS = 1e-6

kernel_name = 'hybrid_retention_rglru_gated_block'


def rmsnorm(x, w):
    xf = x.astype(jnp.float32)
    y = xf * lax.rsqrt(jnp.mean(xf * xf, axis=-1, keepdims=True) + NORM_EPS)
    return (y * w.astype(jnp.float32)).astype(x.dtype)


def rotary(x, pos):
    d = x.shape[-1]
    inv_freq = ROPE_BASE ** (-jnp.arange(0, d, 2, dtype=jnp.float32) / d)
    ang = pos.astype(jnp.float32)[:, None] * inv_freq[None, :]
    cos = jnp.cos(ang)[None, :, None, :]
    sin = jnp.sin(ang)[None, :, None, :]
    x1, x2 = x[..., : d // 2], x[..., d // 2:]
    return jnp.concatenate([x1 * cos - x2 * sin, x2 * cos + x1 * sin], axis=-1)


def chunk_retention(q, k, v):
    B, T, H, dk = q.shape
    dv = v.shape[-1]
    C = RET_CHUNK
    pad = C - N_META
    padw = ((0, 0), (pad, 0), (0, 0), (0, 0))
    q, k, v = jnp.pad(q, padw), jnp.pad(k, padw), jnp.pad(v, padw)
    Tp = T + pad
    n_chunks = Tp // C
    log_g = jnp.log(1.0 - 2.0 ** (-5.0 - jnp.arange(H, dtype=jnp.float32)))
    idx = jnp.arange(C, dtype=jnp.float32)
    diff = idx[:, None] - idx[None, :]
    intra = jnp.where(diff[None] >= 0,
                      jnp.exp(jnp.maximum(diff, 0.0)[None] * log_g[:, None, None]), 0.0)
    q_decay = jnp.exp((idx + 1.0)[:, None] * log_g[None, :])
    k_decay = jnp.exp((C - 1.0 - idx)[:, None] * log_g[None, :])
    chunk_decay = jnp.exp(C * log_g)

    def to_chunks(a):
        return a.reshape(B, n_chunks, C, H, a.shape[-1]).transpose(1, 0, 2, 3, 4)

    def step(state, qkv):
        qc, kc, vc = qkv
        s = jnp.einsum('bchd,bmhd->bhcm', qc, kc) * intra[None]
        inner = jnp.einsum('bhcm,bmhe->bche', s, vc)
        cross = jnp.einsum('bchd,bhde->bche', qc, state) * q_decay[None, :, :, None]
        state = state * chunk_decay[None, :, None, None] + jnp.einsum(
            'bmhd,bmhe->bhde', kc * k_decay[None, :, :, None], vc)
        return state, inner + cross

    s0 = jnp.zeros((B, H, dk, dv), jnp.float32)
    _, ys = lax.scan(step, s0, (to_chunks(q), to_chunks(k), to_chunks(v)))
    out = ys.transpose(1, 0, 2, 3, 4).reshape(B, Tp, H, dv)
    return out[:, pad:]


def causal_depthwise_conv(x, w, b):
    T = x.shape[1]
    xp = jnp.pad(x, ((0, 0), (CONV_WIDTH - 1, 0), (0, 0)))
    y = b[None, None, :]
    for j in range(CONV_WIDTH):
        y = y + xp[:, j:j + T] * w[j][None, None, :]
    return y


def rg_lru(x, wa, ba, wx, bx, lam):
    B, T, W = x.shape
    xb = x.reshape(B, T, LRU_BLOCKS, LRU_BLOCK)
    r = jax.nn.sigmoid(jnp.einsum('btgi,gij->btgj', xb, wa).reshape(B, T, W) + ba)
    i = jax.nn.sigmoid(jnp.einsum('btgi,gij->btgj', xb, wx).reshape(B, T, W) + bx)
    log_a = -LRU_C * r * jax.nn.softplus(-lam)
    a = jnp.exp(log_a)
    u = jnp.sqrt(-jnp.expm1(2.0 * log_a)) * (i * x)

    def step(h, au):
        a_t, u_t = au
        h = a_t * h + u_t
        return h, h

    _, hs = lax.scan(step, jnp.zeros((B, W), jnp.float32),
                     (a.transpose(1, 0, 2), u.transpose(1, 0, 2)))
    return hs.transpose(1, 0, 2)


def setup_inputs(seed: int = 0) -> dict:
    key = jax.random.key(seed)
    ks = jax.random.split(key, 20)
    f32 = jnp.float32
    nrm = lambda k, s, sc: jax.random.normal(k, s, f32) * sc
    a0 = jax.random.uniform(ks[9], (DEPTH, LRU_WIDTH), f32, minval=0.9, maxval=0.999)
    a0r = a0 ** (1.0 / LRU_C)
    return {
        'x': nrm(ks[0], (BATCH, SEQ, D_MODEL), 1.0),
        'meta_tokens': nrm(ks[1], (N_META, D_MODEL), 1.0),
        'mix_norm_w': 1.0 + nrm(ks[2], (DEPTH, D_MODEL), 0.02),
        'w_in': nrm(ks[3], (DEPTH, D_MODEL, IN_COLS), D_MODEL ** -0.5),
        'conv_w': nrm(ks[4], (DEPTH, CONV_WIDTH, LRU_WIDTH), CONV_WIDTH ** -0.5),
        'conv_b': nrm(ks[5], (DEPTH, LRU_WIDTH), 0.01),
        'lru_wa': nrm(ks[6], (DEPTH, LRU_BLOCKS, LRU_BLOCK, LRU_BLOCK), LRU_BLOCK ** -0.5),
        'lru_ba': nrm(ks[7], (DEPTH, LRU_WIDTH), 0.01),
        'lru_wx': nrm(ks[8], (DEPTH, LRU_BLOCKS, LRU_BLOCK, LRU_BLOCK), LRU_BLOCK ** -0.5),
        'lru_bx': nrm(ks[10], (DEPTH, LRU_WIDTH), 0.01),
        'lru_lambda': jnp.log(a0r) - jnp.log1p(-a0r),
        'w_branch_ret': nrm(ks[11], (DEPTH, RET_V, D_MODEL), RET_V ** -0.5),
        'w_branch_lru': nrm(ks[12], (DEPTH, LRU_WIDTH, D_MODEL), LRU_WIDTH ** -0.5),
        'w_out': nrm(ks[13], (DEPTH, D_MODEL, D_MODEL), D_MODEL ** -0.5),
        'ffn_norm_w': 1.0 + nrm(ks[14], (DEPTH, D_MODEL), 0.02),
        'w_ffn_in': nrm(ks[15], (DEPTH, D_MODEL, 2 * FFN_HIDDEN), D_MODEL ** -0.5),
        'w_ffn_out': nrm(ks[16], (DEPTH, FFN_HIDDEN, D_MODEL), FFN_HIDDEN ** -0.5),
        'final_norm_w': 1.0 + nrm(ks[17], (D_MODEL,), 0.02),
    }


def reference(x, meta_tokens, mix_norm_w, w_in, conv_w, conv_b, lru_wa, lru_ba, lru_wx,
              lru_bx, lru_lambda, w_branch_ret, w_branch_lru, w_out, ffn_norm_w,
              w_ffn_in, w_ffn_out, final_norm_w):
    B = x.shape[0]
    f32 = jnp.float32
    meta = jnp.broadcast_to(meta_tokens.astype(x.dtype)[None], (B, N_META, D_MODEL))
    h = jnp.concatenate([meta, x], axis=1)
    T = h.shape[1]
    pos = jnp.arange(T)
    sizes = (RET_QK, RET_QK, RET_V, RET_V, LRU_WIDTH, LRU_WIDTH, D_MODEL, D_MODEL)
    split_at = [int(s) for s in np.cumsum(sizes)[:-1]]
    for l in range(DEPTH):
        u = rmsnorm(h, mix_norm_w[l])
        proj = u @ w_in[l]
        q, k, v, g_ret, lru_in, lru_gate, gate_a, gate_b = jnp.split(proj, split_at, axis=-1)
        q = rotary(q.reshape(B, T, RET_HEADS, RET_QK_DIM).astype(f32), pos)
        k = rotary(k.reshape(B, T, RET_HEADS, RET_QK_DIM).astype(f32), pos) * (RET_QK_DIM ** -0.5)
        v = v.reshape(B, T, RET_HEADS, RET_V_DIM).astype(f32)
        o = chunk_retention(q, k, v)
        o = o * lax.rsqrt(jnp.mean(o * o, axis=-1, keepdims=True) + NORM_EPS)
        o = o.reshape(B, T, RET_V).astype(h.dtype)
        y_ret = (jax.nn.silu(g_ret) * o) @ w_branch_ret[l]
        c = causal_depthwise_conv(lru_in.astype(f32), conv_w[l].astype(f32), conv_b[l].astype(f32))
        r = rg_lru(c, lru_wa[l].astype(f32), lru_ba[l].astype(f32), lru_wx[l].astype(f32),
                   lru_bx[l].astype(f32), lru_lambda[l].astype(f32)).astype(h.dtype)
        y_lru = (jax.nn.gelu(lru_gate) * r) @ w_branch_lru[l]
        mixed = jax.nn.sigmoid(gate_a) * y_ret + jax.nn.sigmoid(gate_b) * y_lru
        h = h + mixed @ w_out[l]
        u = rmsnorm(h, ffn_norm_w[l])
        gu = u @ w_ffn_in[l]
        g, up = gu[..., :FFN_HIDDEN], gu[..., FFN_HIDDEN:]
        h = h + (jax.nn.silu(g) * up) @ w_ffn_out[l]
    h = rmsnorm(h, final_norm_w)
    return h[:, N_META:]
```

```python
import functools

import numpy as np
import jax
import jax.numpy as jnp
from jax import lax
from jax.experimental import pallas as pl
from jax.experimental.pallas import tpu as pltpu

D_MODEL = 1024
N_META = 16
HEADS = 8
HEAD_DIM = 128
CHUNK = 128
ROPE_BASE = 10000.0
LRU_BLOCKS = 4
LRU_BLOCK = D_MODEL // LRU_BLOCKS
CONV_WIDTH = 4
LRU_C = 8.0
FFN_HIDDEN = 2816
IN_COLS = 8 * D_MODEL
NORM_EPS = 1e-6

COL_Q, COL_K, COL_V, COL_GRET, COL_LIN, COL_LGATE, COL_GA, COL_GB = range(8)

VMEM_LIMIT_BYTES = 56 * 1024 * 1024
PROJ_ROWS = 512
RET_ROWS = 512
LRU_ROWS = 512
TAIL_ROWS = 256
FFN_COLS = 256
HIST = 8

f32 = jnp.float32
bf16 = jnp.bfloat16


def _rmsnorm(x, w):
    return x * lax.rsqrt(jnp.mean(x * x, axis=-1, keepdims=True) + NORM_EPS) * w


def _resident(shape):
    nd = len(shape)
    return pl.BlockSpec(shape, lambda *_: (0,) * nd, pipeline_mode=pl.Buffered(1))


def _proj_kernel(x_ref, nw_ref, w_ref, o_ref):
    u = _rmsnorm(x_ref[...], nw_ref[...]).astype(bf16)
    for j in range(IN_COLS // D_MODEL):
        cols = slice(j * D_MODEL, (j + 1) * D_MODEL)
        o_ref[:, cols] = jnp.dot(u, w_ref[:, cols], preferred_element_type=f32).astype(bf16)


def _proj(x2d, norm_w, w_in, rows):
    n = x2d.shape[0]
    return pl.pallas_call(
        _proj_kernel,
        grid=(n // rows,),
        in_specs=[
            pl.BlockSpec((rows, D_MODEL), lambda i: (i, 0)),
            _resident((1, D_MODEL)),
            _resident((D_MODEL, IN_COLS)),
        ],
        out_specs=pl.BlockSpec((rows, IN_COLS), lambda i: (i, 0)),
        out_shape=jax.ShapeDtypeStruct((n, IN_COLS), bf16),
        compiler_params=pltpu.CompilerParams(
            dimension_semantics=("parallel",), vmem_limit_bytes=VMEM_LIMIT_BYTES),
        name="proj",
    )(x2d, norm_w, w_in)


def _rot(x, cos, sin_signed):
    return x * cos + pltpu.roll(x, HEAD_DIM // 2, axis=1) * sin_signed


def _dot_t0(a, b):
    return lax.dot_general(a, b, (((0,), (0,)), ((), ())), preferred_element_type=f32)


def _dot_t1(a, b):
    return lax.dot_general(a, b, (((1,), (1,)), ((), ())), preferred_element_type=f32)


def _ret_kernel(chunk_decay, q_ref, k_ref, v_ref, g_ref, km_ref, vm_ref,
                cq_ref, sq_ref, ck_ref, sk_ref, ckm_ref, skm_ref,
                intra_ref, qd_ref, kd_ref, o_ref, state_ref):
    t = pl.program_id(1)

    @pl.when(t == 0)
    def _():
        ck = ckm_ref[...]
        sk = skm_ref[...]
        for h in range(HEADS):
            cols = slice(h * HEAD_DIM, (h + 1) * HEAD_DIM)
            kr = _rot(km_ref[:, cols].astype(f32), ck, sk)
            state_ref[h] = _dot_t0((kr * kd_ref[h]).astype(bf16), vm_ref[:, cols])

    def chunk_body(c, carry):
        rows = pl.ds(pl.multiple_of(c * CHUNK, CHUNK), CHUNK)
        cq = cq_ref[rows, :]
        sq = sq_ref[rows, :]
        ck = ck_ref[rows, :]
        sk = sk_ref[rows, :]
        for h in range(HEADS):
            cols = slice(h * HEAD_DIM, (h + 1) * HEAD_DIM)
            qr = _rot(q_ref[rows, cols].astype(f32), cq, sq)
            kr = _rot(k_ref[rows, cols].astype(f32), ck, sk)
            v = v_ref[rows, cols]
            s = _dot_t1(qr.astype(bf16), kr.astype(bf16)) * intra_ref[h]
            state = state_ref[h]
            o = (jnp.dot(s.astype(bf16), v, preferred_element_type=f32)
                 + jnp.dot((qr * qd_ref[h]).astype(bf16), state.astype(bf16),
                           preferred_element_type=f32))
            state_ref[h] = state * chunk_decay[h] + _dot_t0((kr * kd_ref[h]).astype(bf16), v)
            o = o * lax.rsqrt(jnp.mean(o * o, axis=-1, keepdims=True) + NORM_EPS)
            g = g_ref[rows, cols].astype(f32)
            o_ref[rows, cols] = (g * jax.nn.sigmoid(g) * o).astype(bf16)
        return carry

    lax.fori_loop(0, q_ref.shape[0] // CHUNK, chunk_body, 0)


def _retention_tables(seq):
    inv_freq = ROPE_BASE ** (-jnp.arange(0, HEAD_DIM, 2, dtype=f32) / HEAD_DIM)

    def tables(pos):
        ang = pos.astype(f32)[:, None] * inv_freq[None, :]
        cos = jnp.cos(ang)
        sin = jnp.sin(ang)
        return jnp.concatenate([cos, cos], axis=1), jnp.concatenate([-sin, sin], axis=1)

    cq, sq = tables(jnp.arange(seq) + N_META)
    cm, sm = tables(jnp.maximum(jnp.arange(CHUNK) - (CHUNK - N_META), 0))
    scale = HEAD_DIM ** -0.5
    log_g = np.log(1.0 - 2.0 ** (-5.0 - np.arange(HEADS, dtype=np.float64)))
    idx = np.arange(CHUNK, dtype=np.float64)
    diff = idx[:, None] - idx[None, :]
    intra = np.where(diff[None] >= 0, np.exp(np.maximum(diff, 0.0)[None] * log_g[:, None, None]), 0.0)
    ones = np.ones((1, 1, HEAD_DIM))
    qd = np.exp((idx + 1.0)[None, :, None] * log_g[:, None, None]) * ones
    kd = np.exp((CHUNK - 1.0 - idx)[None, :, None] * log_g[:, None, None]) * ones
    chunk_decay = tuple(float(v) for v in np.exp(CHUNK * log_g))
    consts = tuple(jnp.asarray(a, f32) for a in (intra, qd, kd))
    return (cq, sq, cq * scale, sq * scale, cm * scale, sm * scale), consts, chunk_decay


def _retention(proj, proj_meta, batch, seq):
    rot, consts, chunk_decay = _retention_tables(seq)
    cq, sq, ck, sk, ckm, skm = rot
    nt = seq // RET_ROWS

    def col_spec(col):
        return pl.BlockSpec((RET_ROWS, D_MODEL), lambda b, t: (b * nt + t, col))

    def meta_spec(col):
        return pl.BlockSpec((CHUNK, D_MODEL), lambda b, t: (0, col))

    tab_spec = pl.BlockSpec((RET_ROWS, HEAD_DIM), lambda b, t: (t, 0))
    const_spec = _resident((HEADS, CHUNK, HEAD_DIM))
    return pl.pallas_call(
        functools.partial(_ret_kernel, chunk_decay),
        grid=(batch, nt),
        in_specs=[col_spec(COL_Q), col_spec(COL_K), col_spec(COL_V), col_spec(COL_GRET),
                  meta_spec(COL_K), meta_spec(COL_V),
                  tab_spec, tab_spec, tab_spec, tab_spec,
                  _resident((CHUNK, HEAD_DIM)), _resident((CHUNK, HEAD_DIM)),
                  const_spec, const_spec, const_spec],
        out_specs=pl.BlockSpec((RET_ROWS, D_MODEL), lambda b, t: (b * nt + t, 0)),
        out_shape=jax.ShapeDtypeStruct((batch * seq, D_MODEL), bf16),
        scratch_shapes=[pltpu.VMEM((HEADS, HEAD_DIM, HEAD_DIM), f32)],
        compiler_params=pltpu.CompilerParams(
            dimension_semantics=("parallel", "arbitrary"), vmem_limit_bytes=VMEM_LIMIT_BYTES),
        name="retention",
    )(proj, proj, proj, proj, proj_meta, proj_meta, cq, sq, ck, sk, ckm, skm, *consts)


def _scan8(a, u):
    row = lax.broadcasted_iota(jnp.int32, a.shape, 0)
    for d in (1, 2, 4):
        keep = row >= d
        u = u + a * jnp.where(keep, pltpu.roll(u, d, axis=0), 0.0)
        a = a * jnp.where(keep, pltpu.roll(a, d, axis=0), 1.0)
    return a, u


def _gelu_tanh(x):
    return 0.5 * x * (1.0 + jnp.tanh(0.7978845608028654 * (x + 0.044715 * (x * x * x))))


def _lru_rows(x, hist, h0, p, xp_ref, a_ref, u_ref, h_ref):
    cw_ref, cb_ref, wa_ref, ba_ref, wx_ref, bx_ref, lam_ref = p
    nrows = x.shape[0]
    xp_ref[0:HIST, :] = hist
    xp_ref[HIST:HIST + nrows, :] = x
    c = cb_ref[...] + cw_ref[CONV_WIDTH - 1:CONV_WIDTH, :] * x
    for j in range(CONV_WIDTH - 1):
        off = HIST - (CONV_WIDTH - 1) + j
        c = c + cw_ref[j:j + 1, :] * xp_ref[off:off + nrows, :]
    cb = c.astype(bf16)

    def gate(w_ref, b_ref):
        parts = [jnp.dot(cb[:, g * LRU_BLOCK:(g + 1) * LRU_BLOCK], w_ref[g],
                         preferred_element_type=f32) for g in range(LRU_BLOCKS)]
        return jax.nn.sigmoid(jnp.concatenate(parts, axis=1) + b_ref[...])

    r = gate(wa_ref, ba_ref)
    i = gate(wx_ref, bx_ref)
    z = -lam_ref[...]
    softplus = jnp.maximum(z, 0.0) + jnp.log1p(jnp.exp(-jnp.abs(z)))
    log_a = (-LRU_C) * r * softplus
    a = jnp.exp(log_a)
    mult = jnp.sqrt(-jnp.tanh(log_a) * (a * a + 1.0))
    a_ref[0:nrows, :] = a
    u_ref[0:nrows, :] = mult * (i * c)

    def group(gidx, h):
        rows = pl.ds(pl.multiple_of(gidx * 8, 8), 8)
        pa, su = _scan8(a_ref[rows, :], u_ref[rows, :])
        hg = su + pa * h
        h_ref[rows, :] = hg
        return hg[7:8, :]

    h_last = lax.fori_loop(0, nrows // 8, group, h0)
    return h_ref[0:nrows, :], xp_ref[nrows:nrows + HIST, :], h_last


def _lru_kernel(x_ref, gate_ref, xm_ref, cw_ref, cb_ref, wa_ref, ba_ref, wx_ref, bx_ref, lam_ref,
                o_ref, hist_ref, hstate_ref, xp_ref, a_ref, u_ref, h_ref):
    t = pl.program_id(1)
    p = (cw_ref, cb_ref, wa_ref, ba_ref, wx_ref, bx_ref, lam_ref)
    scratch = (xp_ref, a_ref, u_ref, h_ref)

    @pl.when(t == 0)
    def _():
        _, hist, h_last = _lru_rows(xm_ref[...].astype(f32), jnp.zeros((HIST, D_MODEL), f32),
                                    jnp.zeros((1, D_MODEL), f32), p, *scratch)
        hist_ref[...] = hist
        hstate_ref[...] = h_last

    h, hist, h_last = _lru_rows(x_ref[...].astype(f32), hist_ref[...], hstate_ref[...], p, *scratch)
    hist_ref[...] = hist
    hstate_ref[...] = h_last
    o_ref[...] = (_gelu_tanh(gate_ref[...].astype(f32)) * h).astype(bf16)


def _lru(proj, proj_meta, params, batch, seq):
    nt = seq // LRU_ROWS
    meta_block = CHUNK // N_META - 1

    def col_spec(col):
        return pl.BlockSpec((LRU_ROWS, D_MODEL), lambda b, t: (b * nt + t, col))

    in_specs = [col_spec(COL_LIN), col_spec(COL_LGATE),
                pl.BlockSpec((N_META, D_MODEL), lambda b, t: (meta_block, COL_LIN))]
    in_specs += [_resident(a.shape) for a in params]
    return pl.pallas_call(
        _lru_kernel,
        grid=(batch, nt),
        in_specs=in_specs,
        out_specs=pl.BlockSpec((LRU_ROWS, D_MODEL), lambda b, t: (b * nt + t, 0)),
        out_shape=jax.ShapeDtypeStruct((batch * seq, D_MODEL), bf16),
        scratch_shapes=[pltpu.VMEM((HIST, D_MODEL), f32),
                        pltpu.VMEM((1, D_MODEL), f32),
                        pltpu.VMEM((LRU_ROWS + HIST, D_MODEL), f32),
                        pltpu.VMEM((LRU_ROWS, D_MODEL), f32),
                        pltpu.VMEM((LRU_ROWS, D_MODEL), f32),
                        pltpu.VMEM((LRU_ROWS, D_MODEL), f32)],
        compiler_params=pltpu.CompilerParams(
            dimension_semantics=("parallel", "arbitrary"), vmem_limit_bytes=VMEM_LIMIT_BYTES),
        name="rglru",
    )(proj, proj, proj_meta, *params)


def _tail_kernel(x_ref, za_ref, zb_ref, ga_ref, gb_ref, wr_ref, wl_ref, wo_ref, n2_ref,
                 wfi_ref, wfo_ref, n3_ref, o_ref, act_ref):
    ya = jnp.dot(za_ref[...], wr_ref[...], preferred_element_type=f32)
    yl = jnp.dot(zb_ref[...], wl_ref[...], preferred_element_type=f32)
    mixed = (jax.nn.sigmoid(ga_ref[...].astype(f32)) * ya
             + jax.nn.sigmoid(gb_ref[...].astype(f32)) * yl)
    h = x_ref[...] + jnp.dot(mixed.astype(bf16), wo_ref[...], preferred_element_type=f32)
    u = _rmsnorm(h, n2_ref[...]).astype(bf16)
    for j in range(FFN_HIDDEN // FFN_COLS):
        g = jnp.dot(u, wfi_ref[:, j * FFN_COLS:(j + 1) * FFN_COLS], preferred_element_type=f32)
        up = jnp.dot(u, wfi_ref[:, FFN_HIDDEN + j * FFN_COLS:FFN_HIDDEN + (j + 1) * FFN_COLS],
                     preferred_element_type=f32)
        act_ref[:, j * FFN_COLS:(j + 1) * FFN_COLS] = (g * jax.nn.sigmoid(g) * up).astype(bf16)
    h = h + jnp.dot(act_ref[...], wfo_ref[...], preferred_element_type=f32)
    o_ref[...] = _rmsnorm(h, n3_ref[...])


def _tail(x2d, za, zb, proj, weights):
    n = x2d.shape[0]
    row_spec = pl.BlockSpec((TAIL_ROWS, D_MODEL), lambda i: (i, 0))

    def col_spec(col):
        return pl.BlockSpec((TAIL_ROWS, D_MODEL), lambda i: (i, col))

    return pl.pallas_call(
        _tail_kernel,
        grid=(n // TAIL_ROWS,),
        in_specs=[row_spec, row_spec, row_spec, col_spec(COL_GA), col_spec(COL_GB)]
                 + [_resident(w.shape) for w in weights],
        out_specs=row_spec,
        out_shape=jax.ShapeDtypeStruct((n, D_MODEL), f32),
        scratch_shapes=[pltpu.VMEM((TAIL_ROWS, FFN_HIDDEN), bf16)],
        compiler_params=pltpu.CompilerParams(
            dimension_semantics=("parallel",), vmem_limit_bytes=VMEM_LIMIT_BYTES),
        name="merge_ffn",
    )(x2d, za, zb, proj, proj, *weights)


def kernel(x, meta_tokens, mix_norm_w, w_in, conv_w, conv_b, lru_wa, lru_ba, lru_wx, lru_bx,
           lru_lambda, w_branch_ret, w_branch_lru, w_out, ffn_norm_w, w_ffn_in, w_ffn_out,
           final_norm_w):
    batch, seq, d = x.shape
    assert d == D_MODEL and w_in.shape == (1, D_MODEL, IN_COLS), "single-layer block only"
    assert meta_tokens.shape == (N_META, D_MODEL)
    assert seq % RET_ROWS == 0 and seq % LRU_ROWS == 0 and (batch * seq) % PROJ_ROWS == 0

    row = lambda a: a.reshape(1, -1).astype(f32)
    x2d = x.reshape(batch * seq, D_MODEL)
    meta_pad = jnp.concatenate(
        [jnp.zeros((CHUNK - N_META, D_MODEL), x.dtype), meta_tokens.astype(x.dtype)], axis=0)
    w_in_b = w_in[0].astype(bf16)
    norm1 = row(mix_norm_w[0])

    proj = _proj(x2d, norm1, w_in_b, PROJ_ROWS)
    proj_meta = _proj(meta_pad, norm1, w_in_b, CHUNK)

    za = _retention(proj, proj_meta, batch, seq)
    lru_params = (conv_w[0].astype(f32), row(conv_b[0]), lru_wa[0].astype(bf16), row(lru_ba[0]),
                  lru_wx[0].astype(bf16), row(lru_bx[0]), row(lru_lambda[0]))
    zb = _lru(proj, proj_meta, lru_params, batch, seq)

    weights = (w_branch_ret[0].astype(bf16), w_branch_lru[0].astype(bf16), w_out[0].astype(bf16),
               row(ffn_norm_w[0]), w_ffn_in[0].astype(bf16), w_ffn_out[0].astype(bf16),
               row(final_norm_w))
    out = _tail(x2d, za, zb, proj, weights)
    return out.reshape(batch, seq, D_MODEL)
```

```python
import functools

import numpy as np
import jax
import jax.numpy as jnp
from jax import lax
from jax.experimental import pallas as pl
from jax.experimental.pallas import tpu as pltpu

D_MODEL = 1024
N_META = 16
HEADS = 8
HEAD_DIM = 128
CHUNK = 128
ROPE_BASE = 10000.0
LRU_BLOCKS = 4
LRU_BLOCK = D_MODEL // LRU_BLOCKS
CONV_WIDTH = 4
LRU_C = 8.0
FFN_HIDDEN = 2816
NORM_EPS = 1e-6

COL_Q, COL_K, COL_V, COL_GRET, COL_LIN, COL_LGATE, COL_GA, COL_GB = range(8)

SUBLANES = 8
LANES = 128
VMEM_LIMIT_BYTES = 56 * 1024 * 1024
TAIL_ROWS = 256
FFN_COLS = 256
PITCH = CHUNK + SUBLANES
HIST_STEPS = CONV_WIDTH - 1

f32 = jnp.float32
bf16 = jnp.bfloat16


def _rmsnorm(x, w):
    return x * lax.rsqrt(jnp.mean(x * x, axis=-1, keepdims=True) + NORM_EPS) * w


def _dot(a, b):
    return jnp.dot(a, b, preferred_element_type=f32)


def _resident(shape):
    nd = len(shape)
    return pl.BlockSpec(shape, lambda *_: (0,) * nd, pipeline_mode=pl.Buffered(1))


def _wcols(col):
    return pl.BlockSpec((D_MODEL, D_MODEL), lambda *_: (0, col), pipeline_mode=pl.Buffered(1))


def _rot(x, cos, sin_signed):
    return x * cos + pltpu.roll(x, HEAD_DIM // 2, axis=1) * sin_signed


def _dot_t0(a, b):
    return lax.dot_general(a, b, (((0,), (0,)), ((), ())), preferred_element_type=f32)


def _dot_t1(a, b):
    return lax.dot_general(a, b, (((1,), (1,)), ((), ())), preferred_element_type=f32)


def _ret_kernel(chunk_decay, x_ref, meta_ref, nw_ref, wq_ref, wk_ref, wv_ref, wg_ref,
                cq_ref, sq_ref, ck_ref, sk_ref, ckm_ref, skm_ref, intra_ref, qd_ref, kd_ref,
                o_ref, state_ref):
    batch = x_ref.shape[0]
    t = pl.program_id(0)

    @pl.when(t == 0)
    def _():
        um = _rmsnorm(meta_ref[...], nw_ref[...]).astype(bf16)
        km = _dot(um, wk_ref[...])
        vm = _dot(um, wv_ref[...]).astype(bf16)
        ck = ckm_ref[...]
        sk = skm_ref[...]
        for h in range(HEADS):
            cols = slice(h * HEAD_DIM, (h + 1) * HEAD_DIM)
            kr = _rot(km[:, cols], ck, sk)
            st = _dot_t0((kr * kd_ref[h]).astype(bf16), vm[:, cols])
            for b in range(batch):
                state_ref[b, h] = st

    u = _rmsnorm(x_ref[...].reshape(batch * CHUNK, D_MODEL), nw_ref[...]).astype(bf16)
    cq = cq_ref[...]
    sq = sq_ref[...]
    ck = ck_ref[...]
    sk = sk_ref[...]
    pair = 2 * HEAD_DIM
    for hp in range(HEADS // 2):
        pcols = slice(hp * pair, (hp + 1) * pair)
        q2 = _dot(u, wq_ref[:, pcols])
        k2 = _dot(u, wk_ref[:, pcols])
        v2 = _dot(u, wv_ref[:, pcols]).astype(bf16)
        g2 = _dot(u, wg_ref[:, pcols])
        for b in range(batch):
            rows = slice(b * CHUNK, (b + 1) * CHUNK)
            for hh in range(2):
                h = 2 * hp + hh
                lanes = slice(hh * HEAD_DIM, (hh + 1) * HEAD_DIM)
                qr = _rot(q2[rows, lanes], cq, sq)
                kr = _rot(k2[rows, lanes], ck, sk)
                v = v2[rows, lanes]
                s = _dot_t1(qr.astype(bf16), kr.astype(bf16)) * intra_ref[h]
                state = state_ref[b, h]
                o = (_dot(s.astype(bf16), v)
                     + _dot((qr * qd_ref[h]).astype(bf16), state.astype(bf16)))
                state_ref[b, h] = (state * chunk_decay[h]
                                   + _dot_t0((kr * kd_ref[h]).astype(bf16), v))
                o = o * lax.rsqrt(jnp.mean(o * o, axis=-1, keepdims=True) + NORM_EPS)
                g = g2[rows, lanes]
                o_ref[b, :, h * HEAD_DIM:(h + 1) * HEAD_DIM] = (
                    g * jax.nn.sigmoid(g) * o).astype(bf16)


def _retention_tables(seq):
    inv_freq = ROPE_BASE ** (-jnp.arange(0, HEAD_DIM, 2, dtype=f32) / HEAD_DIM)

    def tables(pos):
        ang = pos.astype(f32)[:, None] * inv_freq[None, :]
        cos = jnp.cos(ang)
        sin = jnp.sin(ang)
        return jnp.concatenate([cos, cos], axis=1), jnp.concatenate([-sin, sin], axis=1)

    cq, sq = tables(jnp.arange(seq) + N_META)
    cm, sm = tables(jnp.maximum(jnp.arange(CHUNK) - (CHUNK - N_META), 0))
    scale = HEAD_DIM ** -0.5
    log_g = np.log(1.0 - 2.0 ** (-5.0 - np.arange(HEADS, dtype=np.float64)))
    idx = np.arange(CHUNK, dtype=np.float64)
    diff = idx[:, None] - idx[None, :]
    intra = np.where(diff[None] >= 0, np.exp(np.maximum(diff, 0.0)[None] * log_g[:, None, None]), 0.0)
    ones = np.ones((1, 1, HEAD_DIM))
    qd = np.exp((idx + 1.0)[None, :, None] * log_g[:, None, None]) * ones
    kd = np.exp((CHUNK - 1.0 - idx)[None, :, None] * log_g[:, None, None]) * ones
    chunk_decay = tuple(float(v) for v in np.exp(CHUNK * log_g))
    consts = tuple(jnp.asarray(a, f32) for a in (intra, qd, kd))
    return (cq, sq, cq * scale, sq * scale, cm * scale, sm * scale), consts, chunk_decay


def _retention(x, meta_pad, norm_w, w_in):
    batch, seq, _ = x.shape
    rot, consts, chunk_decay = _retention_tables(seq)
    tile_spec = pl.BlockSpec((batch, CHUNK, D_MODEL), lambda t: (0, t, 0))
    tab_spec = pl.BlockSpec((CHUNK, HEAD_DIM), lambda t: (t, 0))
    const_spec = _resident((HEADS, CHUNK, HEAD_DIM))
    return pl.pallas_call(
        functools.partial(_ret_kernel, chunk_decay),
        grid=(seq // CHUNK,),
        in_specs=[tile_spec, _resident((CHUNK, D_MODEL)), _resident((1, D_MODEL)),
                  _wcols(COL_Q), _wcols(COL_K), _wcols(COL_V), _wcols(COL_GRET),
                  tab_spec, tab_spec, tab_spec, tab_spec,
                  _resident((CHUNK, HEAD_DIM)), _resident((CHUNK, HEAD_DIM)),
                  const_spec, const_spec, const_spec],
        out_specs=tile_spec,
        out_shape=jax.ShapeDtypeStruct((batch, seq, D_MODEL), bf16),
        scratch_shapes=[pltpu.VMEM((batch, HEADS, HEAD_DIM, HEAD_DIM), f32)],
        compiler_params=pltpu.CompilerParams(
            dimension_semantics=("arbitrary",), vmem_limit_bytes=VMEM_LIMIT_BYTES),
        name="retention",
    )(x, meta_pad, norm_w, w_in, w_in, w_in, w_in, *rot, *consts)


def _gelu_tanh(x):
    return 0.5 * x * (1.0 + jnp.tanh(x * (0.7978845608028654 + 0.035677408136300125 * (x * x))))


def _lru_group(x_tm, hist, h, p, g):
    cw_ref, cb_ref, wa_ref, ba_ref, wx_ref, bx_ref, lam_ref = p
    cols = slice(g * LRU_BLOCK, (g + 1) * LRU_BLOCK)
    n = x_tm.shape[0]
    x_ext = jnp.concatenate([hist, x_tm], axis=0)
    c = cb_ref[:, cols] + cw_ref[CONV_WIDTH - 1:CONV_WIDTH, cols] * x_tm
    for j in range(CONV_WIDTH - 1):
        c = c + cw_ref[j:j + 1, cols] * x_ext[j * SUBLANES:j * SUBLANES + n]
    cb = c.astype(bf16)
    r = jax.nn.sigmoid(_dot(cb, wa_ref[g]) + ba_ref[:, cols])
    i = jax.nn.sigmoid(_dot(cb, wx_ref[g]) + bx_ref[:, cols])
    z = -lam_ref[:, cols]
    decay = LRU_C * (jnp.maximum(z, 0.0) + jnp.log1p(jnp.exp(-jnp.abs(z))))
    neg_log_a = r * decay
    a = jnp.exp2(r * (decay * (-1.4426950408889634)))
    y = jnp.tanh(neg_log_a) * (a * a + 1.0)
    mult = jnp.where(y > 0.0, y * lax.rsqrt(y), 0.0)
    u = mult * (i * c)
    states = []
    for step in range(n // SUBLANES):
        rows = slice(step * SUBLANES, (step + 1) * SUBLANES)
        h = a[rows] * h + u[rows]
        states.append(h)
    return states, x_ext[n:]


def _lru_kernel(x_ref, meta_ref, nw_ref, wl_ref, wg_ref, cw_ref, cb_ref, wa_ref, ba_ref, wx_ref,
                bx_ref, lam_ref, o_ref, hist_ref, hstate_ref, lin_ref, hout_ref):
    batch = x_ref.shape[0]
    assert batch == SUBLANES
    t = pl.program_id(0)
    p = (cw_ref, cb_ref, wa_ref, ba_ref, wx_ref, bx_ref, lam_ref)
    slabs = LRU_BLOCK // LANES

    @pl.when(t == 0)
    def _():
        um = _rmsnorm(meta_ref[...], nw_ref[...]).astype(bf16)
        for g in range(LRU_BLOCKS):
            cols = slice(g * LRU_BLOCK, (g + 1) * LRU_BLOCK)
            lin = _dot(um, wl_ref[:, cols])
            states, hist = _lru_group(lin, jnp.zeros((HIST_STEPS * SUBLANES, LRU_BLOCK), f32),
                                      jnp.zeros((SUBLANES, LRU_BLOCK), f32), p, g)
            hist_ref[:, cols] = hist
            hstate_ref[:, cols] = states[-1]

    u = _rmsnorm(x_ref[...].reshape(batch * CHUNK, D_MODEL), nw_ref[...]).astype(bf16)
    for g in range(LRU_BLOCKS):
        cols = slice(g * LRU_BLOCK, (g + 1) * LRU_BLOCK)
        lin = _dot(u, wl_ref[:, cols])
        for b in range(batch):
            for s in range(slabs):
                lin_ref[g * slabs + s, b * PITCH:b * PITCH + CHUNK, :] = (
                    lin[b * CHUNK:(b + 1) * CHUNK, s * LANES:(s + 1) * LANES])
        x_tm = jnp.concatenate(
            [jnp.concatenate([lin_ref[g * slabs + s, pl.ds(step, SUBLANES, stride=PITCH), :]
                              for step in range(CHUNK)], axis=0) for s in range(slabs)], axis=1)
        states, hist = _lru_group(x_tm, hist_ref[:, cols], hstate_ref[:, cols], p, g)
        hist_ref[:, cols] = hist
        hstate_ref[:, cols] = states[-1]
        for step, h in enumerate(states):
            for s in range(slabs):
                hout_ref[g * slabs + s, pl.ds(step, SUBLANES, stride=PITCH), :] = (
                    h[:, s * LANES:(s + 1) * LANES])
        gate = _gelu_tanh(_dot(u, wg_ref[:, cols]))
        for b in range(batch):
            hb = jnp.concatenate([hout_ref[g * slabs + s, b * PITCH:b * PITCH + CHUNK, :]
                                  for s in range(slabs)], axis=1)
            o_ref[b, :, cols] = (gate[b * CHUNK:(b + 1) * CHUNK] * hb).astype(bf16)


def _lru(x, meta_tm, norm_w, w_in, params):
    batch, seq, _ = x.shape
    tile_spec = pl.BlockSpec((batch, CHUNK, D_MODEL), lambda t: (0, t, 0))
    n_slabs = D_MODEL // LANES
    return pl.pallas_call(
        _lru_kernel,
        grid=(seq // CHUNK,),
        in_specs=[tile_spec, _resident(meta_tm.shape), _resident((1, D_MODEL)),
                  _wcols(COL_LIN), _wcols(COL_LGATE)] + [_resident(a.shape) for a in params],
        out_specs=tile_spec,
        out_shape=jax.ShapeDtypeStruct((batch, seq, D_MODEL), bf16),
        scratch_shapes=[pltpu.VMEM((HIST_STEPS * SUBLANES, D_MODEL), f32),
                        pltpu.VMEM((SUBLANES, D_MODEL), f32),
                        pltpu.VMEM((n_slabs, batch * PITCH, LANES), f32),
                        pltpu.VMEM((n_slabs, batch * PITCH, LANES), f32)],
        compiler_params=pltpu.CompilerParams(
            dimension_semantics=("arbitrary",), vmem_limit_bytes=VMEM_LIMIT_BYTES),
        name="rglru",
    )(x, meta_tm, norm_w, w_in, w_in, *params)


def _tail_kernel(x_ref, za_ref, zb_ref, n1_ref, wga_ref, wgb_ref, wr_ref, wl_ref, wo_ref, n2_ref,
                 wfi_ref, wfo_ref, n3_ref, o_ref, act_ref):
    x = x_ref[...]
    u1 = _rmsnorm(x, n1_ref[...]).astype(bf16)
    mixed = (jax.nn.sigmoid(_dot(u1, wga_ref[...])) * _dot(za_ref[...], wr_ref[...])
             + jax.nn.sigmoid(_dot(u1, wgb_ref[...])) * _dot(zb_ref[...], wl_ref[...]))
    h = x + _dot(mixed.astype(bf16), wo_ref[...])
    u = _rmsnorm(h, n2_ref[...]).astype(bf16)
    for j in range(FFN_HIDDEN // FFN_COLS):
        g = _dot(u, wfi_ref[:, j * FFN_COLS:(j + 1) * FFN_COLS])
        up = _dot(u, wfi_ref[:, FFN_HIDDEN + j * FFN_COLS:FFN_HIDDEN + (j + 1) * FFN_COLS])
        act_ref[:, j * FFN_COLS:(j + 1) * FFN_COLS] = (g * jax.nn.sigmoid(g) * up).astype(bf16)
    h = h + _dot(act_ref[...], wfo_ref[...])
    o_ref[...] = _rmsnorm(h, n3_ref[...])


def _tail(x2d, za, zb, norm1, w_in, weights):
    n = x2d.shape[0]
    row_spec = pl.BlockSpec((TAIL_ROWS, D_MODEL), lambda i: (i, 0))
    return pl.pallas_call(
        _tail_kernel,
        grid=(n // TAIL_ROWS,),
        in_specs=[row_spec, row_spec, row_spec, _resident((1, D_MODEL)),
                  _wcols(COL_GA), _wcols(COL_GB)] + [_resident(w.shape) for w in weights],
        out_specs=row_spec,
        out_shape=jax.ShapeDtypeStruct((n, D_MODEL), f32),
        scratch_shapes=[pltpu.VMEM((TAIL_ROWS, FFN_HIDDEN), bf16)],
        compiler_params=pltpu.CompilerParams(
            dimension_semantics=("parallel",), vmem_limit_bytes=VMEM_LIMIT_BYTES),
        name="merge_ffn",
    )(x2d, za, zb, norm1, w_in, w_in, *weights)


def kernel(x, meta_tokens, mix_norm_w, w_in, conv_w, conv_b, lru_wa, lru_ba, lru_wx, lru_bx,
           lru_lambda, w_branch_ret, w_branch_lru, w_out, ffn_norm_w, w_ffn_in, w_ffn_out,
           final_norm_w):
    batch, seq, d = x.shape
    assert d == D_MODEL and w_in.shape == (1, D_MODEL, 8 * D_MODEL), "single-layer block only"
    assert meta_tokens.shape == (N_META, D_MODEL)
    assert batch == SUBLANES and seq % CHUNK == 0 and (batch * seq) % TAIL_ROWS == 0

    row = lambda a: a.reshape(1, -1).astype(f32)
    meta = meta_tokens.astype(f32)
    meta_pad = jnp.concatenate([jnp.zeros((CHUNK - N_META, D_MODEL), f32), meta], axis=0)
    meta_tm = jnp.repeat(meta, SUBLANES, axis=0)
    w_in_b = w_in[0].astype(bf16)
    norm1 = row(mix_norm_w[0])

    za = _retention(x, meta_pad, norm1, w_in_b)
    lru_params = (conv_w[0].astype(f32), row(conv_b[0]), lru_wa[0].astype(bf16), row(lru_ba[0]),
                  lru_wx[0].astype(bf16), row(lru_bx[0]), row(lru_lambda[0]))
    zb = _lru(x, meta_tm, norm1, w_in_b, lru_params)

    weights = (w_branch_ret[0].astype(bf16), w_branch_lru[0].astype(bf16), w_out[0].astype(bf16),
               row(ffn_norm_w[0]), w_ffn_in[0].astype(bf16), w_ffn_out[0].astype(bf16),
               row(final_norm_w))
    n = batch * seq
    out = _tail(x.reshape(n, D_MODEL), za.reshape(n, D_MODEL), zb.reshape(n, D_MODEL),
                norm1, w_in_b, weights)
    return out.reshape(batch, seq, D_MODEL)
```

```python
import functools

import numpy as np
import jax
import jax.numpy as jnp
from jax import lax
from jax.experimental import pallas as pl
from jax.experimental.pallas import tpu as pltpu

D_MODEL = 1024
N_META = 16
HEADS = 8
HEAD_DIM = 128
CHUNK = 128
ROPE_BASE = 10000.0
LRU_BLOCKS = 4
LRU_BLOCK = D_MODEL // LRU_BLOCKS
CONV_WIDTH = 4
LRU_C = 8.0
FFN_HIDDEN = 2816
NORM_EPS = 1e-6

COL_Q, COL_K, COL_V, COL_GRET, COL_LIN, COL_LGATE, COL_GA, COL_GB = range(8)

SUBLANES = 8
LANES = 128
VMEM_LIMIT_BYTES = 56 * 1024 * 1024
TAIL_ROWS = 512
TAIL_SUB_ROWS = 256
FFN_COLS = 256
PITCH = CHUNK + SUBLANES
HIST_STEPS = CONV_WIDTH - 1

f32 = jnp.float32
bf16 = jnp.bfloat16


def _rmsnorm(x, w):
    return x * lax.rsqrt(jnp.mean(x * x, axis=-1, keepdims=True) + NORM_EPS) * w


def _dot(a, b):
    return jnp.dot(a, b, preferred_element_type=f32)


def _resident(shape):
    nd = len(shape)
    return pl.BlockSpec(shape, lambda *_: (0,) * nd, pipeline_mode=pl.Buffered(1))


def _wcols(col):
    return pl.BlockSpec((D_MODEL, D_MODEL), lambda *_: (0, col), pipeline_mode=pl.Buffered(1))


def _rot(x, cos, sin_signed):
    return x * cos + pltpu.roll(x, HEAD_DIM // 2, axis=1) * sin_signed


def _dot_t0(a, b):
    return lax.dot_general(a, b, (((0,), (0,)), ((), ())), preferred_element_type=f32)


def _dot_t1(a, b):
    return lax.dot_general(a, b, (((1,), (1,)), ((), ())), preferred_element_type=f32)


def _ret_kernel(chunk_decay, x_ref, meta_ref, nw_ref, wq_ref, wk_ref, wv_ref, wg_ref,
                cq_ref, sq_ref, ck_ref, sk_ref, ckm_ref, skm_ref, intra_ref, qd_ref, kd_ref,
                o_ref, state_ref):
    batch = x_ref.shape[0]
    t = pl.program_id(0)

    @pl.when(t == 0)
    def _():
        um = _rmsnorm(meta_ref[...], nw_ref[...]).astype(bf16)
        km = _dot(um, wk_ref[...])
        vm = _dot(um, wv_ref[...]).astype(bf16)
        ck = ckm_ref[...]
        sk = skm_ref[...]
        for h in range(HEADS):
            cols = slice(h * HEAD_DIM, (h + 1) * HEAD_DIM)
            kr = _rot(km[:, cols], ck, sk)
            st = _dot_t0((kr * kd_ref[h]).astype(bf16), vm[:, cols])
            for b in range(batch):
                state_ref[b, h] = st

    u = _rmsnorm(x_ref[...].reshape(batch * CHUNK, D_MODEL), nw_ref[...]).astype(bf16)
    cq = cq_ref[...]
    sq = sq_ref[...]
    ck = ck_ref[...]
    sk = sk_ref[...]
    pair = 2 * HEAD_DIM
    for hp in range(HEADS // 2):
        pcols = slice(hp * pair, (hp + 1) * pair)
        q2 = _dot(u, wq_ref[:, pcols])
        k2 = _dot(u, wk_ref[:, pcols])
        v2 = _dot(u, wv_ref[:, pcols]).astype(bf16)
        g2 = _dot(u, wg_ref[:, pcols])
        items = [(b, hh) for b in range(batch) for hh in range(2)]
        sub = lambda a, b, hh: a[b * CHUNK:(b + 1) * CHUNK, hh * HEAD_DIM:(hh + 1) * HEAD_DIM]
        qr, kr, s = {}, {}, {}
        for it in items:
            qr[it] = _rot(sub(q2, *it), cq, sq)
            kr[it] = _rot(sub(k2, *it), ck, sk)
            s[it] = _dot_t1(qr[it].astype(bf16), kr[it].astype(bf16))
        o = {}
        for it in items:
            b, hh = it
            h = 2 * hp + hh
            v = sub(v2, *it)
            state = state_ref[b, h]
            lhs = jnp.concatenate([(s[it] * intra_ref[h]).astype(bf16),
                                   (qr[it] * qd_ref[h]).astype(bf16)], axis=1)
            rhs = jnp.concatenate([v, state.astype(bf16)], axis=0)
            o[it] = _dot(lhs, rhs)
            state_ref[b, h] = (state * chunk_decay[h]
                               + _dot_t0((kr[it] * kd_ref[h]).astype(bf16), v))
        for it in items:
            b, hh = it
            h = 2 * hp + hh
            on = o[it] * lax.rsqrt(jnp.mean(o[it] * o[it], axis=-1, keepdims=True) + NORM_EPS)
            g = sub(g2, *it)
            o_ref[b, :, h * HEAD_DIM:(h + 1) * HEAD_DIM] = (
                g * jax.nn.sigmoid(g) * on).astype(bf16)


def _retention_tables(seq):
    inv_freq = ROPE_BASE ** (-jnp.arange(0, HEAD_DIM, 2, dtype=f32) / HEAD_DIM)

    def tables(pos):
        ang = pos.astype(f32)[:, None] * inv_freq[None, :]
        cos = jnp.cos(ang)
        sin = jnp.sin(ang)
        return jnp.concatenate([cos, cos], axis=1), jnp.concatenate([-sin, sin], axis=1)

    cq, sq = tables(jnp.arange(seq) + N_META)
    cm, sm = tables(jnp.maximum(jnp.arange(CHUNK) - (CHUNK - N_META), 0))
    scale = HEAD_DIM ** -0.5
    log_g = np.log(1.0 - 2.0 ** (-5.0 - np.arange(HEADS, dtype=np.float64)))
    idx = np.arange(CHUNK, dtype=np.float64)
    diff = idx[:, None] - idx[None, :]
    intra = np.where(diff[None] >= 0, np.exp(np.maximum(diff, 0.0)[None] * log_g[:, None, None]), 0.0)
    ones = np.ones((1, 1, HEAD_DIM))
    qd = np.exp((idx + 1.0)[None, :, None] * log_g[:, None, None]) * ones
    kd = np.exp((CHUNK - 1.0 - idx)[None, :, None] * log_g[:, None, None]) * ones
    chunk_decay = tuple(float(v) for v in np.exp(CHUNK * log_g))
    consts = tuple(jnp.asarray(a, f32) for a in (intra, qd, kd))
    return (cq, sq, cq * scale, sq * scale, cm * scale, sm * scale), consts, chunk_decay


def _retention(x, meta_pad, norm_w, w_in):
    batch, seq, _ = x.shape
    rot, consts, chunk_decay = _retention_tables(seq)
    tile_spec = pl.BlockSpec((batch, CHUNK, D_MODEL), lambda t: (0, t, 0))
    tab_spec = pl.BlockSpec((CHUNK, HEAD_DIM), lambda t: (t, 0))
    const_spec = _resident((HEADS, CHUNK, HEAD_DIM))
    return pl.pallas_call(
        functools.partial(_ret_kernel, chunk_decay),
        grid=(seq // CHUNK,),
        in_specs=[tile_spec, _resident((CHUNK, D_MODEL)), _resident((1, D_MODEL)),
                  _wcols(COL_Q), _wcols(COL_K), _wcols(COL_V), _wcols(COL_GRET),
                  tab_spec, tab_spec, tab_spec, tab_spec,
                  _resident((CHUNK, HEAD_DIM)), _resident((CHUNK, HEAD_DIM)),
                  const_spec, const_spec, const_spec],
        out_specs=tile_spec,
        out_shape=jax.ShapeDtypeStruct((batch, seq, D_MODEL), bf16),
        scratch_shapes=[pltpu.VMEM((batch, HEADS, HEAD_DIM, HEAD_DIM), f32)],
        compiler_params=pltpu.CompilerParams(
            dimension_semantics=("arbitrary",), vmem_limit_bytes=VMEM_LIMIT_BYTES),
        name="retention",
    )(x, meta_pad, norm_w, w_in, w_in, w_in, w_in, *rot, *consts)


def _gelu_tanh(x):
    k0 = -2.0 * 0.7978845608028654
    k1 = -2.0 * 0.035677408136300125
    return x / (1.0 + jnp.exp(x * (k0 + k1 * (x * x))))


def _lru_gates(x_tm, hist, p, g):
    cw_ref, cb_ref, wax_ref, bax_ref, lam_ref = p
    cols = slice(g * LRU_BLOCK, (g + 1) * LRU_BLOCK)
    n = x_tm.shape[0]
    x_ext = jnp.concatenate([hist, x_tm], axis=0)
    c = cb_ref[:, cols] + cw_ref[CONV_WIDTH - 1:CONV_WIDTH, cols] * x_tm
    for j in range(CONV_WIDTH - 1):
        c = c + cw_ref[j:j + 1, cols] * x_ext[j * SUBLANES:j * SUBLANES + n]
    ri = jax.nn.sigmoid(_dot(c.astype(bf16), wax_ref[g]) + bax_ref[g])
    r = ri[:, :LRU_BLOCK]
    i = ri[:, LRU_BLOCK:]
    z = -lam_ref[:, cols]
    decay = LRU_C * (jnp.maximum(z, 0.0) + jnp.log1p(jnp.exp(-jnp.abs(z))))
    neg_log_a = r * decay
    a = jnp.exp(-neg_log_a)
    y = jnp.tanh(neg_log_a) * (a * a + 1.0)
    mult = jnp.where(y > 0.0, y * lax.rsqrt(y), 0.0)
    return a, mult * (i * c), x_ext[n:]


def _lru_scan(a, u, h):
    states = []
    for step in range(a.shape[0] // SUBLANES):
        rows = slice(step * SUBLANES, (step + 1) * SUBLANES)
        h = a[rows] * h + u[rows]
        states.append(h)
    return states


def _lru_kernel(x_ref, meta_ref, nw_ref, wl_ref, wg_ref, cw_ref, cb_ref, wax_ref, bax_ref,
                lam_ref, o_ref, hist_ref, hstate_ref, lin_ref, hout_ref):
    batch = x_ref.shape[0]
    assert batch == SUBLANES
    t = pl.program_id(0)
    p = (cw_ref, cb_ref, wax_ref, bax_ref, lam_ref)
    slabs = LRU_BLOCK // LANES
    half = D_MODEL // 2

    @pl.when(t == 0)
    def _():
        lin = _dot(_rmsnorm(meta_ref[...], nw_ref[...]).astype(bf16), wl_ref[...])
        for g in range(LRU_BLOCKS):
            cols = slice(g * LRU_BLOCK, (g + 1) * LRU_BLOCK)
            a, uin, hist = _lru_gates(lin[:, cols],
                                      jnp.zeros((HIST_STEPS * SUBLANES, LRU_BLOCK), f32), p, g)
            hist_ref[:, cols] = hist
            hstate_ref[:, cols] = _lru_scan(a, uin, jnp.zeros((SUBLANES, LRU_BLOCK), f32))[-1]

    u = _rmsnorm(x_ref[...].reshape(batch * CHUNK, D_MODEL), nw_ref[...]).astype(bf16)
    lin = _dot(u, wl_ref[...])
    for b in range(batch):
        for s in range(D_MODEL // LANES):
            lin_ref[s, b * PITCH:b * PITCH + CHUNK, :] = (
                lin[b * CHUNK:(b + 1) * CHUNK, s * LANES:(s + 1) * LANES])

    def recurrence(g):
        cols = slice(g * LRU_BLOCK, (g + 1) * LRU_BLOCK)
        x_tm = jnp.concatenate(
            [jnp.concatenate([lin_ref[g * slabs + s, pl.ds(step, SUBLANES, stride=PITCH), :]
                              for step in range(CHUNK)], axis=0) for s in range(slabs)], axis=1)
        a, uin, hist = _lru_gates(x_tm, hist_ref[:, cols], p, g)
        hist_ref[:, cols] = hist
        states = _lru_scan(a, uin, hstate_ref[:, cols])
        hstate_ref[:, cols] = states[-1]
        for step, h in enumerate(states):
            for s in range(slabs):
                hout_ref[g * slabs + s, pl.ds(step, SUBLANES, stride=PITCH), :] = (
                    h[:, s * LANES:(s + 1) * LANES])

    def emit(c0):
        gate = _gelu_tanh(_dot(u, wg_ref[:, c0:c0 + half]))
        for b in range(batch):
            hb = jnp.concatenate([hout_ref[(c0 // LANES) + s, b * PITCH:b * PITCH + CHUNK, :]
                                  for s in range(half // LANES)], axis=1)
            o_ref[b, :, c0:c0 + half] = (gate[b * CHUNK:(b + 1) * CHUNK] * hb).astype(bf16)

    recurrence(0)
    recurrence(1)
    emit(0)
    recurrence(2)
    recurrence(3)
    emit(half)


def _lru(x, meta_tm, norm_w, w_in, params):
    batch, seq, _ = x.shape
    tile_spec = pl.BlockSpec((batch, CHUNK, D_MODEL), lambda t: (0, t, 0))
    n_slabs = D_MODEL // LANES
    return pl.pallas_call(
        _lru_kernel,
        grid=(seq // CHUNK,),
        in_specs=[tile_spec, _resident(meta_tm.shape), _resident((1, D_MODEL)),
                  _wcols(COL_LIN), _wcols(COL_LGATE)] + [_resident(a.shape) for a in params],
        out_specs=tile_spec,
        out_shape=jax.ShapeDtypeStruct((batch, seq, D_MODEL), bf16),
        scratch_shapes=[pltpu.VMEM((HIST_STEPS * SUBLANES, D_MODEL), f32),
                        pltpu.VMEM((SUBLANES, D_MODEL), f32),
                        pltpu.VMEM((n_slabs, batch * PITCH, LANES), f32),
                        pltpu.VMEM((n_slabs, batch * PITCH, LANES), f32)],
        compiler_params=pltpu.CompilerParams(
            dimension_semantics=("arbitrary",), vmem_limit_bytes=VMEM_LIMIT_BYTES),
        name="rglru",
    )(x, meta_tm, norm_w, w_in, w_in, *params)


def _tail_kernel(x_ref, za_ref, zb_ref, n1_ref, wga_ref, wgb_ref, wr_ref, wl_ref, wo_ref, n2_ref,
                 wfi_ref, wfo_ref, n3_ref, o_ref, act_ref):
    n_sub = x_ref.shape[0] // TAIL_SUB_ROWS
    subs = [slice(i * TAIL_SUB_ROWS, (i + 1) * TAIL_SUB_ROWS) for i in range(n_sub)]
    x = [x_ref[r, :] for r in subs]
    u1 = [_rmsnorm(xi, n1_ref[...]).astype(bf16) for xi in x]
    mixed = [jax.nn.sigmoid(_dot(u1[i], wga_ref[...])) * _dot(za_ref[r, :], wr_ref[...])
             + jax.nn.sigmoid(_dot(u1[i], wgb_ref[...])) * _dot(zb_ref[r, :], wl_ref[...])
             for i, r in enumerate(subs)]
    h = [x[i] + _dot(mixed[i].astype(bf16), wo_ref[...]) for i in range(n_sub)]
    u = [_rmsnorm(hi, n2_ref[...]).astype(bf16) for hi in h]
    for j in range(FFN_HIDDEN // FFN_COLS):
        for i, r in enumerate(subs):
            g = _dot(u[i], wfi_ref[:, j * FFN_COLS:(j + 1) * FFN_COLS])
            up = _dot(u[i], wfi_ref[:, FFN_HIDDEN + j * FFN_COLS:FFN_HIDDEN + (j + 1) * FFN_COLS])
            act_ref[r, j * FFN_COLS:(j + 1) * FFN_COLS] = (
                g * jax.nn.sigmoid(g) * up).astype(bf16)
    for i, r in enumerate(subs):
        o_ref[r, :] = _rmsnorm(h[i] + _dot(act_ref[r, :], wfo_ref[...]), n3_ref[...])


def _tail(x2d, za, zb, norm1, w_in, weights):
    n = x2d.shape[0]
    row_spec = pl.BlockSpec((TAIL_ROWS, D_MODEL), lambda i: (i, 0))
    return pl.pallas_call(
        _tail_kernel,
        grid=(n // TAIL_ROWS,),
        in_specs=[row_spec, row_spec, row_spec, _resident((1, D_MODEL)),
                  _wcols(COL_GA), _wcols(COL_GB)] + [_resident(w.shape) for w in weights],
        out_specs=row_spec,
        out_shape=jax.ShapeDtypeStruct((n, D_MODEL), f32),
        scratch_shapes=[pltpu.VMEM((TAIL_ROWS, FFN_HIDDEN), bf16)],
        compiler_params=pltpu.CompilerParams(
            dimension_semantics=("parallel",), vmem_limit_bytes=VMEM_LIMIT_BYTES),
        name="merge_ffn",
    )(x2d, za, zb, norm1, w_in, w_in, *weights)


def kernel(x, meta_tokens, mix_norm_w, w_in, conv_w, conv_b, lru_wa, lru_ba, lru_wx, lru_bx,
           lru_lambda, w_branch_ret, w_branch_lru, w_out, ffn_norm_w, w_ffn_in, w_ffn_out,
           final_norm_w):
    batch, seq, d = x.shape
    assert d == D_MODEL and w_in.shape == (1, D_MODEL, 8 * D_MODEL), "single-layer block only"
    assert meta_tokens.shape == (N_META, D_MODEL)
    assert batch == SUBLANES and seq % CHUNK == 0 and (batch * seq) % TAIL_ROWS == 0

    row = lambda a: a.reshape(1, -1).astype(f32)
    meta = meta_tokens.astype(f32)
    meta_pad = jnp.concatenate([jnp.zeros((CHUNK - N_META, D_MODEL), f32), meta], axis=0)
    meta_tm = jnp.repeat(meta, SUBLANES, axis=0)
    w_in_b = w_in[0].astype(bf16)
    norm1 = row(mix_norm_w[0])

    za = _retention(x, meta_pad, norm1, w_in_b)
    wax = jnp.concatenate([lru_wa[0], lru_wx[0]], axis=-1).astype(bf16)
    bax = jnp.concatenate([lru_ba[0].reshape(LRU_BLOCKS, 1, LRU_BLOCK),
                           lru_bx[0].reshape(LRU_BLOCKS, 1, LRU_BLOCK)], axis=-1).astype(f32)
    lru_params = (conv_w[0].astype(f32), row(conv_b[0]), wax, bax, row(lru_lambda[0]))
    zb = _lru(x, meta_tm, norm1, w_in_b, lru_params)

    weights = (w_branch_ret[0].astype(bf16), w_branch_lru[0].astype(bf16), w_out[0].astype(bf16),
               row(ffn_norm_w[0]), w_ffn_in[0].astype(bf16), w_ffn_out[0].astype(bf16),
               row(final_norm_w))
    n = batch * seq
    out = _tail(x.reshape(n, D_MODEL), za.reshape(n, D_MODEL), zb.reshape(n, D_MODEL),
                norm1, w_in_b, weights)
    return out.reshape(batch, seq, D_MODEL)
```

```python
import functools

import numpy as np
import jax
import jax.numpy as jnp
from jax import lax
from jax.experimental import pallas as pl
from jax.experimental.pallas import tpu as pltpu

D_MODEL = 1024
N_META = 16
HEADS = 8
HEAD_DIM = 128
CHUNK = 128
ROPE_BASE = 10000.0
LRU_BLOCKS = 4
LRU_BLOCK = D_MODEL // LRU_BLOCKS
CONV_WIDTH = 4
LRU_C = 8.0
FFN_HIDDEN = 2816
NORM_EPS = 1e-6

COL_Q, COL_K, COL_V, COL_GRET, COL_LIN, COL_LGATE, COL_GA, COL_GB = range(8)

SUBLANES = 8
LANES = 128
VMEM_LIMIT_BYTES = 60 * 1024 * 1024
TAIL_ROWS = 512
TAIL_SUB_ROWS = 256
FFN_COLS = 256
PITCH = CHUNK + SUBLANES
HIST_STEPS = CONV_WIDTH - 1
LRU_ROW_BLOCKS = 4
PAIR = 2 * HEAD_DIM

f32 = jnp.float32
bf16 = jnp.bfloat16


def _rmsnorm(x, w):
    return x * lax.rsqrt(jnp.mean(x * x, axis=-1, keepdims=True) + NORM_EPS) * w


def _dot(a, b):
    return jnp.dot(a, b, preferred_element_type=f32)


def _resident(shape):
    nd = len(shape)
    return pl.BlockSpec(shape, lambda *_: (0,) * nd, pipeline_mode=pl.Buffered(1))


def _wcols(col):
    return pl.BlockSpec((D_MODEL, D_MODEL), lambda *_: (0, col), pipeline_mode=pl.Buffered(1))


def _rot(x, cos, sin_signed):
    return x * cos + pltpu.roll(x, HEAD_DIM // 2, axis=1) * sin_signed


def _dot_t0(a, b):
    return lax.dot_general(a, b, (((0,), (0,)), ((), ())), preferred_element_type=f32)


def _dot_t1(a, b):
    return lax.dot_general(a, b, (((1,), (1,)), ((), ())), preferred_element_type=f32)


def _retention_tables(seq):
    inv_freq = ROPE_BASE ** (-jnp.arange(0, HEAD_DIM, 2, dtype=f32) / HEAD_DIM)

    def tables(pos):
        ang = pos.astype(f32)[:, None] * inv_freq[None, :]
        cos = jnp.cos(ang)
        sin = jnp.sin(ang)
        return jnp.concatenate([cos, cos], axis=1), jnp.concatenate([-sin, sin], axis=1)

    cq, sq = tables(jnp.arange(seq) + N_META)
    cm, sm = tables(jnp.maximum(jnp.arange(CHUNK) - (CHUNK - N_META), 0))
    scale = HEAD_DIM ** -0.5
    log_g = np.log(1.0 - 2.0 ** (-5.0 - np.arange(HEADS, dtype=np.float64)))
    idx = np.arange(CHUNK, dtype=np.float64)
    diff = idx[:, None] - idx[None, :]
    intra = np.where(diff[None] >= 0, np.exp(np.maximum(diff, 0.0)[None] * log_g[:, None, None]), 0.0)
    ones = np.ones((1, 1, HEAD_DIM))
    qd = np.exp((idx + 1.0)[None, :, None] * log_g[:, None, None]) * ones
    kd = np.exp((CHUNK - 1.0 - idx)[None, :, None] * log_g[:, None, None]) * ones
    chunk_decay = tuple(float(v) for v in np.exp(CHUNK * log_g))
    consts = tuple(jnp.asarray(a, f32) for a in (intra, qd, kd))
    return (cq, sq, cq * scale, sq * scale, cm * scale, sm * scale), consts, chunk_decay


def _gelu_tanh(x):
    k0 = -2.0 * 0.7978845608028654
    k1 = -2.0 * 0.035677408136300125
    return x / (1.0 + jnp.exp(x * (k0 + k1 * (x * x))))


def _lru_gates(x_tm, prev, p, g):
    cw_ref, cb_ref, wax_ref, bax_ref, lam_ref = p
    cols = slice(g * LRU_BLOCK, (g + 1) * LRU_BLOCK)
    n = x_tm.shape[0]
    x_ext = jnp.concatenate([prev, x_tm], axis=0)
    c = cb_ref[:, cols] + cw_ref[CONV_WIDTH - 1:CONV_WIDTH, cols] * x_tm
    for j in range(CONV_WIDTH - 1):
        c = c + cw_ref[j:j + 1, cols] * x_ext[j * SUBLANES:j * SUBLANES + n]
    ri = jax.nn.sigmoid(_dot(c.astype(bf16), wax_ref[g]) + bax_ref[g])
    r = ri[:, :LRU_BLOCK]
    i = ri[:, LRU_BLOCK:]
    z = -lam_ref[:, cols]
    decay = LRU_C * (jnp.maximum(z, 0.0) + jnp.log1p(jnp.exp(-jnp.abs(z))))
    neg_log_a = r * decay
    a = jnp.exp(-neg_log_a)
    y = jnp.tanh(neg_log_a) * (a * a + 1.0)
    mult = jnp.where(y > 0.0, y * lax.rsqrt(y), 0.0)
    return a, mult * (i * c), x_ext[n:]


def _lru_scan(a, u, h):
    states = []
    for step in range(a.shape[0] // SUBLANES):
        rows = slice(step * SUBLANES, (step + 1) * SUBLANES)
        h = a[rows] * h + u[rows]
        states.append(h)
    return states


def _mixer_kernel(chunk_decay, x_ref, metap_ref, metat_ref, nw_ref,
                  wq_ref, wk_ref, wv_ref, wg_ref, wl_ref, wlg_ref,
                  cq_ref, sq_ref, ck_ref, sk_ref, ckm_ref, skm_ref, intra_ref, qd_ref, kd_ref,
                  cw_ref, cb_ref, wax_ref, bax_ref, lam_ref,
                  za_ref, zb_ref, state_ref, hist_ref, hstate_ref, stage_ref):
    batch = x_ref.shape[0]
    assert batch == SUBLANES
    t = pl.program_id(0)
    p = (cw_ref, cb_ref, wax_ref, bax_ref, lam_ref)
    slabs = LRU_BLOCK // LANES
    half = D_MODEL // 2

    @pl.when(t == 0)
    def _():
        um = _rmsnorm(metap_ref[...], nw_ref[...]).astype(bf16)
        km = _dot(um, wk_ref[...])
        vm = _dot(um, wv_ref[...]).astype(bf16)
        for h in range(HEADS):
            cols = slice(h * HEAD_DIM, (h + 1) * HEAD_DIM)
            kr = _rot(km[:, cols], ckm_ref[...], skm_ref[...])
            st = _dot_t0((kr * kd_ref[h]).astype(bf16), vm[:, cols])
            for b in range(batch):
                state_ref[b, h] = st
        lin = _dot(_rmsnorm(metat_ref[...], nw_ref[...]).astype(bf16), wl_ref[...])
        for g in range(LRU_BLOCKS):
            cols = slice(g * LRU_BLOCK, (g + 1) * LRU_BLOCK)
            a, uin, hist = _lru_gates(lin[:, cols],
                                      jnp.zeros((HIST_STEPS * SUBLANES, LRU_BLOCK), f32), p, g)
            hist_ref[:, cols] = hist
            hstate_ref[:, cols] = _lru_scan(a, uin, jnp.zeros((SUBLANES, LRU_BLOCK), f32))[-1]

    u = _rmsnorm(x_ref[...].reshape(batch * CHUNK, D_MODEL), nw_ref[...]).astype(bf16)
    cq = cq_ref[...]
    sq = sq_ref[...]
    ck = ck_ref[...]
    sk = sk_ref[...]

    def zip_tasks(*task_lists):
        n = max(len(tl) for tl in task_lists)
        for k in range(n):
            for tl in task_lists:
                for task in tl[k * len(tl) // n:(k + 1) * len(tl) // n]:
                    task()

    def lru_project(b):
        lin = _dot(u[b * CHUNK:(b + 1) * CHUNK], wl_ref[...])
        for s in range(D_MODEL // LANES):
            stage_ref[s, b * PITCH:b * PITCH + CHUNK, :] = lin[:, s * LANES:(s + 1) * LANES]

    lru = {}
    steps_per_block = CHUNK // LRU_ROW_BLOCKS

    def lru_rows(g, rb):
        cols = slice(g * LRU_BLOCK, (g + 1) * LRU_BLOCK)
        steps = range(rb * steps_per_block, (rb + 1) * steps_per_block)
        x_tm = jnp.concatenate(
            [jnp.concatenate([stage_ref[g * slabs + s, pl.ds(step, SUBLANES, stride=PITCH), :]
                              for step in steps], axis=0) for s in range(slabs)], axis=1)
        prev = hist_ref[:, cols] if rb == 0 else lru[g, rb - 1][2]
        lru[g, rb] = _lru_gates(x_tm, prev, p, g)
        if rb == LRU_ROW_BLOCKS - 1:
            hist_ref[:, cols] = lru[g, rb][2]

    def lru_recurrence(g):
        cols = slice(g * LRU_BLOCK, (g + 1) * LRU_BLOCK)
        h = hstate_ref[:, cols]
        for rb in range(LRU_ROW_BLOCKS):
            a, uin, _ = lru[g, rb]
            for i, h in enumerate(_lru_scan(a, uin, h)):
                step = rb * steps_per_block + i
                for s in range(slabs):
                    stage_ref[g * slabs + s, pl.ds(step, SUBLANES, stride=PITCH), :] = (
                        h[:, s * LANES:(s + 1) * LANES])
        hstate_ref[:, cols] = h

    def lru_emit(c0, b):
        gate = _gelu_tanh(_dot(u[b * CHUNK:(b + 1) * CHUNK], wlg_ref[:, c0:c0 + half]))
        hb = jnp.concatenate([stage_ref[(c0 // LANES) + s, b * PITCH:b * PITCH + CHUNK, :]
                              for s in range(half // LANES)], axis=1)
        zb_ref[b, :, c0:c0 + half] = (gate * hb).astype(bf16)

    items = [(b, hh) for b in range(batch) for hh in range(2)]
    ret = {}

    def ret_project(hp, bp):
        rows = slice(bp * 2 * CHUNK, (bp + 1) * 2 * CHUNK)
        pcols = slice(hp * PAIR, (hp + 1) * PAIR)
        ret[hp, bp] = (_dot(u[rows], wq_ref[:, pcols]), _dot(u[rows], wk_ref[:, pcols]),
                       _dot(u[rows], wv_ref[:, pcols]).astype(bf16),
                       _dot(u[rows], wg_ref[:, pcols]))

    def sub(hp, which, b, hh):
        a = ret[hp, b // 2][which]
        return a[(b % 2) * CHUNK:(b % 2 + 1) * CHUNK, hh * HEAD_DIM:(hh + 1) * HEAD_DIM]

    def ret_scores(hp, it):
        qr = _rot(sub(hp, 0, *it), cq, sq)
        kr = _rot(sub(hp, 1, *it), ck, sk)
        ret[hp, it] = (qr, kr, _dot_t1(qr.astype(bf16), kr.astype(bf16)))

    def ret_outputs(hp, it):
        b, hh = it
        h = 2 * hp + hh
        qr, kr, s = ret[hp, it]
        v = sub(hp, 2, *it)
        state = state_ref[b, h]
        lhs = jnp.concatenate([(s * intra_ref[h]).astype(bf16),
                               (qr * qd_ref[h]).astype(bf16)], axis=1)
        rhs = jnp.concatenate([v, state.astype(bf16)], axis=0)
        ret[hp, it] = _dot(lhs, rhs)
        state_ref[b, h] = (state * chunk_decay[h]
                           + _dot_t0((kr * kd_ref[h]).astype(bf16), v))

    def ret_emit(hp, it):
        b, hh = it
        h = 2 * hp + hh
        o = ret[hp, it]
        on = o * lax.rsqrt(jnp.mean(o * o, axis=-1, keepdims=True) + NORM_EPS)
        g = sub(hp, 3, *it)
        za_ref[b, :, h * HEAD_DIM:(h + 1) * HEAD_DIM] = (
            g * jax.nn.sigmoid(g) * on).astype(bf16)

    def tasks(fn, *fixed, over):
        return [functools.partial(fn, *fixed, k) for k in over]

    zip_tasks(tasks(lru_project, over=range(batch)))
    for hp in range(HEADS // 2):
        zip_tasks(tasks(ret_project, hp, over=range(batch // 2)),
                  tasks(lru_rows, hp, over=range(LRU_ROW_BLOCKS)))
        zip_tasks(tasks(ret_scores, hp, over=items))
        lru_recurrence(hp)
        zip_tasks(tasks(ret_outputs, hp, over=items))
        if hp % 2 == 1:
            zip_tasks(tasks(ret_emit, hp, over=items),
                      tasks(lru_emit, (hp // 2) * half, over=range(batch)))
        else:
            zip_tasks(tasks(ret_emit, hp, over=items))


def _mixer(x, meta_pad, meta_tm, norm_w, w_in, lru_params):
    batch, seq, _ = x.shape
    rot, consts, chunk_decay = _retention_tables(seq)
    tile_spec = pl.BlockSpec((batch, CHUNK, D_MODEL), lambda t: (0, t, 0))
    tab_spec = pl.BlockSpec((CHUNK, HEAD_DIM), lambda t: (t, 0))
    const_spec = _resident((HEADS, CHUNK, HEAD_DIM))
    out = jax.ShapeDtypeStruct((batch, seq, D_MODEL), bf16)
    return pl.pallas_call(
        functools.partial(_mixer_kernel, chunk_decay),
        grid=(seq // CHUNK,),
        in_specs=[tile_spec, _resident(meta_pad.shape), _resident(meta_tm.shape),
                  _resident((1, D_MODEL)),
                  _wcols(COL_Q), _wcols(COL_K), _wcols(COL_V), _wcols(COL_GRET),
                  _wcols(COL_LIN), _wcols(COL_LGATE),
                  tab_spec, tab_spec, tab_spec, tab_spec,
                  _resident((CHUNK, HEAD_DIM)), _resident((CHUNK, HEAD_DIM)),
                  const_spec, const_spec, const_spec]
                 + [_resident(a.shape) for a in lru_params],
        out_specs=[tile_spec, tile_spec],
        out_shape=[out, out],
        scratch_shapes=[pltpu.VMEM((batch, HEADS, HEAD_DIM, HEAD_DIM), f32),
                        pltpu.VMEM((HIST_STEPS * SUBLANES, D_MODEL), f32),
                        pltpu.VMEM((SUBLANES, D_MODEL), f32),
                        pltpu.VMEM((D_MODEL // LANES, batch * PITCH, LANES), f32)],
        compiler_params=pltpu.CompilerParams(
            dimension_semantics=("arbitrary",), vmem_limit_bytes=VMEM_LIMIT_BYTES),
        name="mixer",
    )(x, meta_pad, meta_tm, norm_w, w_in, w_in, w_in, w_in, w_in, w_in, *rot, *consts,
      *lru_params)


def _tail_kernel(x_ref, za_ref, zb_ref, n1_ref, wga_ref, wgb_ref, wr_ref, wl_ref, wo_ref, n2_ref,
                 wfi_ref, wfo_ref, n3_ref, o_ref, act_ref):
    n_sub = x_ref.shape[0] // TAIL_SUB_ROWS
    subs = [slice(i * TAIL_SUB_ROWS, (i + 1) * TAIL_SUB_ROWS) for i in range(n_sub)]
    x = [x_ref[r, :] for r in subs]
    u1 = [_rmsnorm(xi, n1_ref[...]).astype(bf16) for xi in x]
    mixed = [jax.nn.sigmoid(_dot(u1[i], wga_ref[...])) * _dot(za_ref[r, :], wr_ref[...])
             + jax.nn.sigmoid(_dot(u1[i], wgb_ref[...])) * _dot(zb_ref[r, :], wl_ref[...])
             for i, r in enumerate(subs)]
    h = [x[i] + _dot(mixed[i].astype(bf16), wo_ref[...]) for i in range(n_sub)]
    u = [_rmsnorm(hi, n2_ref[...]).astype(bf16) for hi in h]
    for j in range(FFN_HIDDEN // FFN_COLS):
        for i, r in enumerate(subs):
            g = _dot(u[i], wfi_ref[:, j * FFN_COLS:(j + 1) * FFN_COLS])
            up = _dot(u[i], wfi_ref[:, FFN_HIDDEN + j * FFN_COLS:FFN_HIDDEN + (j + 1) * FFN_COLS])
            act_ref[r, j * FFN_COLS:(j + 1) * FFN_COLS] = (
                g * jax.nn.sigmoid(g) * up).astype(bf16)
    for i, r in enumerate(subs):
        o_ref[r, :] = _rmsnorm(h[i] + _dot(act_ref[r, :], wfo_ref[...]), n3_ref[...])


def _tail(x2d, za, zb, norm1, w_in, weights):
    n = x2d.shape[0]
    row_spec = pl.BlockSpec((TAIL_ROWS, D_MODEL), lambda i: (i, 0))
    return pl.pallas_call(
        _tail_kernel,
        grid=(n // TAIL_ROWS,),
        in_specs=[row_spec, row_spec, row_spec, _resident((1, D_MODEL)),
                  _wcols(COL_GA), _wcols(COL_GB)] + [_resident(w.shape) for w in weights],
        out_specs=row_spec,
        out_shape=jax.ShapeDtypeStruct((n, D_MODEL), f32),
        scratch_shapes=[pltpu.VMEM((TAIL_ROWS, FFN_HIDDEN), bf16)],
        compiler_params=pltpu.CompilerParams(
            dimension_semantics=("parallel",), vmem_limit_bytes=VMEM_LIMIT_BYTES),
        name="merge_ffn",
    )(x2d, za, zb, norm1, w_in, w_in, *weights)


def kernel(x, meta_tokens, mix_norm_w, w_in, conv_w, conv_b, lru_wa, lru_ba, lru_wx, lru_bx,
           lru_lambda, w_branch_ret, w_branch_lru, w_out, ffn_norm_w, w_ffn_in, w_ffn_out,
           final_norm_w):
    batch, seq, d = x.shape
    assert d == D_MODEL and w_in.shape == (1, D_MODEL, 8 * D_MODEL), "single-layer block only"
    assert meta_tokens.shape == (N_META, D_MODEL)
    assert batch == SUBLANES and seq % CHUNK == 0 and (batch * seq) % TAIL_ROWS == 0

    row = lambda a: a.reshape(1, -1).astype(f32)
    meta = meta_tokens.astype(f32)
    meta_pad = jnp.concatenate([jnp.zeros((CHUNK - N_META, D_MODEL), f32), meta], axis=0)
    meta_tm = jnp.repeat(meta, SUBLANES, axis=0)
    w_in_b = w_in[0].astype(bf16)
    norm1 = row(mix_norm_w[0])

    wax = jnp.concatenate([lru_wa[0], lru_wx[0]], axis=-1).astype(bf16)
    bax = jnp.concatenate([lru_ba[0].reshape(LRU_BLOCKS, 1, LRU_BLOCK),
                           lru_bx[0].reshape(LRU_BLOCKS, 1, LRU_BLOCK)], axis=-1).astype(f32)
    lru_params = (conv_w[0].astype(f32), row(conv_b[0]), wax, bax, row(lru_lambda[0]))
    za, zb = _mixer(x, meta_pad, meta_tm, norm1, w_in_b, lru_params)

    weights = (w_branch_ret[0].astype(bf16), w_branch_lru[0].astype(bf16), w_out[0].astype(bf16),
               row(ffn_norm_w[0]), w_ffn_in[0].astype(bf16), w_ffn_out[0].astype(bf16),
               row(final_norm_w))
    n = batch * seq
    out = _tail(x.reshape(n, D_MODEL), za.reshape(n, D_MODEL), zb.reshape(n, D_MODEL),
                norm1, w_in_b, weights)
    return out.reshape(batch, seq, D_MODEL)
```

```python
import functools

import numpy as np
import jax
import jax.numpy as jnp
from jax import lax
from jax.experimental import pallas as pl
from jax.experimental.pallas import tpu as pltpu

D_MODEL = 1024
N_META = 16
HEADS = 8
HEAD_DIM = 128
CHUNK = 128
ROPE_BASE = 10000.0
LRU_BLOCKS = 4
LRU_BLOCK = D_MODEL // LRU_BLOCKS
CONV_WIDTH = 4
LRU_C = 8.0
FFN_HIDDEN = 2816
NORM_EPS = 1e-6

COL_Q, COL_K, COL_V, COL_GRET, COL_LIN, COL_LGATE, COL_GA, COL_GB = range(8)

SUBLANES = 8
LANES = 128
VMEM_LIMIT_BYTES = 60 * 1024 * 1024
TAIL_ROWS = 512
TAIL_SUB_ROWS = 256
FFN_COLS = 256
PITCH = CHUNK + SUBLANES
HIST_STEPS = CONV_WIDTH - 1
LRU_ROW_BLOCKS = 4
ROW_CHUNK = 64
PAIR = 2 * HEAD_DIM

f32 = jnp.float32
bf16 = jnp.bfloat16


def _rmsnorm(x, w):
    return x * lax.rsqrt(jnp.mean(x * x, axis=-1, keepdims=True) + NORM_EPS) * w


def _dot(a, b):
    return jnp.dot(a, b, preferred_element_type=f32)


def _resident(shape):
    nd = len(shape)
    return pl.BlockSpec(shape, lambda *_: (0,) * nd, pipeline_mode=pl.Buffered(1))


def _wcols(col):
    return pl.BlockSpec((D_MODEL, D_MODEL), lambda *_: (0, col), pipeline_mode=pl.Buffered(1))


def _rot(x, cos, sin_signed):
    return x * cos + pltpu.roll(x, HEAD_DIM // 2, axis=1) * sin_signed


def _dot_t0(a, b):
    return lax.dot_general(a, b, (((0,), (0,)), ((), ())), preferred_element_type=f32)


def _dot_t1(a, b):
    return lax.dot_general(a, b, (((1,), (1,)), ((), ())), preferred_element_type=f32)


def _retention_tables(seq):
    inv_freq = ROPE_BASE ** (-np.arange(0, HEAD_DIM, 2, dtype=np.float64) / HEAD_DIM)

    def tables(pos):
        ang = pos.astype(np.float64)[:, None] * inv_freq[None, :]
        cos = np.cos(ang)
        sin = np.sin(ang)
        return (jnp.asarray(np.concatenate([cos, cos], axis=1), f32),
                jnp.asarray(np.concatenate([-sin, sin], axis=1), f32))

    cq, sq = tables(np.arange(seq) + N_META)
    cm, sm = tables(np.maximum(np.arange(CHUNK) - (CHUNK - N_META), 0))
    scale = HEAD_DIM ** -0.5
    log_g = np.log(1.0 - 2.0 ** (-5.0 - np.arange(HEADS, dtype=np.float64)))
    idx = np.arange(CHUNK, dtype=np.float64)
    diff = idx[:, None] - idx[None, :]
    intra = np.where(diff[None] >= 0, np.exp(np.maximum(diff, 0.0)[None] * log_g[:, None, None]), 0.0)
    ones = np.ones((1, 1, HEAD_DIM))
    qd = np.exp((idx + 1.0)[None, :, None] * log_g[:, None, None]) * ones
    kd = np.exp((CHUNK - 1.0 - idx)[None, :, None] * log_g[:, None, None]) * ones
    chunk_decay = tuple(float(v) for v in np.exp(CHUNK * log_g))
    consts = tuple(jnp.asarray(a, f32) for a in (intra * scale, qd, kd * scale))
    return (cq, sq, cm, sm), consts, chunk_decay


def _gelu_tanh(x):
    k0 = -2.0 * 0.7978845608028654
    k1 = -2.0 * 0.035677408136300125
    return x / (1.0 + jnp.exp(x * (k0 + k1 * (x * x))))


def _lru_gates(x_tm, prev, p, g):
    cw_ref, cb_ref, wax_ref, bax_ref, lam_ref = p
    cols = slice(g * LRU_BLOCK, (g + 1) * LRU_BLOCK)
    n = x_tm.shape[0]
    x_ext = jnp.concatenate([prev, x_tm], axis=0)
    chunks = [slice(r0, r0 + ROW_CHUNK) for r0 in range(0, n, ROW_CHUNK)]
    c = []
    for rows in chunks:
        cc = cb_ref[:, cols] + cw_ref[CONV_WIDTH - 1:CONV_WIDTH, cols] * x_tm[rows]
        for j in range(CONV_WIDTH - 1):
            cc = cc + cw_ref[j:j + 1, cols] * x_ext[j * SUBLANES + rows.start:
                                                    j * SUBLANES + rows.stop]
        c.append(cc)
    pre = _dot(jnp.concatenate(c, axis=0).astype(bf16), wax_ref[g])
    z = -lam_ref[:, cols]
    decay = LRU_C * (jnp.maximum(z, 0.0) + jnp.log1p(jnp.exp(-jnp.abs(z))))
    a_parts, u_parts = [], []
    for cc, rows in zip(c, chunks):
        ri = jax.nn.sigmoid(pre[rows] + bax_ref[g])
        r = ri[:, :LRU_BLOCK]
        i = ri[:, LRU_BLOCK:]
        neg_log_a = r * decay
        a = jnp.exp(-neg_log_a)
        y = jnp.tanh(neg_log_a) * (a * a + 1.0)
        mult = jnp.where(y > 0.0, y * lax.rsqrt(y), 0.0)
        a_parts.append(a)
        u_parts.append(mult * (i * cc))
    return jnp.concatenate(a_parts, axis=0), jnp.concatenate(u_parts, axis=0), x_ext[n:]


def _lru_scan(a, u, h):
    states = []
    for step in range(a.shape[0] // SUBLANES):
        rows = slice(step * SUBLANES, (step + 1) * SUBLANES)
        h = a[rows] * h + u[rows]
        states.append(h)
    return states


def _mixer_kernel(chunk_decay, x_ref, metap_ref, metat_ref, nw_ref,
                  wq_ref, wk_ref, wv_ref, wg_ref, wl_ref, wlg_ref,
                  cq_ref, sq_ref, cm_ref, sm_ref, intra_ref, qd_ref, kd_ref,
                  cw_ref, cb_ref, wax_ref, bax_ref, lam_ref,
                  za_ref, zb_ref, state_ref, hist_ref, hstate_ref, stage_ref):
    batch = x_ref.shape[0]
    assert batch == SUBLANES
    t = pl.program_id(0)
    p = (cw_ref, cb_ref, wax_ref, bax_ref, lam_ref)
    slabs = LRU_BLOCK // LANES
    half = D_MODEL // 2

    @pl.when(t == 0)
    def _():
        um = _rmsnorm(metap_ref[...], nw_ref[...]).astype(bf16)
        km = _dot(um, wk_ref[...])
        vm = _dot(um, wv_ref[...]).astype(bf16)
        for h in range(HEADS):
            cols = slice(h * HEAD_DIM, (h + 1) * HEAD_DIM)
            kr = _rot(km[:, cols], cm_ref[...], sm_ref[...])
            st = _dot_t0((kr * kd_ref[h]).astype(bf16), vm[:, cols])
            for b in range(batch):
                state_ref[b, h] = st
        lin = _dot(_rmsnorm(metat_ref[...], nw_ref[...]).astype(bf16), wl_ref[...])
        for g in range(LRU_BLOCKS):
            cols = slice(g * LRU_BLOCK, (g + 1) * LRU_BLOCK)
            a, uin, hist = _lru_gates(lin[:, cols],
                                      jnp.zeros((HIST_STEPS * SUBLANES, LRU_BLOCK), f32), p, g)
            hist_ref[:, cols] = hist
            hstate_ref[:, cols] = _lru_scan(a, uin, jnp.zeros((SUBLANES, LRU_BLOCK), f32))[-1]

    u = _rmsnorm(x_ref[...].reshape(batch * CHUNK, D_MODEL), nw_ref[...]).astype(bf16)
    cq = cq_ref[...]
    sq = sq_ref[...]

    def zip_tasks(*task_lists):
        n = max(len(tl) for tl in task_lists)
        for k in range(n):
            for tl in task_lists:
                for task in tl[k * len(tl) // n:(k + 1) * len(tl) // n]:
                    task()

    def lru_project(b):
        lin = _dot(u[b * CHUNK:(b + 1) * CHUNK], wl_ref[...])
        for s in range(D_MODEL // LANES):
            stage_ref[s, b * PITCH:b * PITCH + CHUNK, :] = lin[:, s * LANES:(s + 1) * LANES]

    lru = {}
    steps_per_block = CHUNK // LRU_ROW_BLOCKS

    def lru_rows(g, rb):
        cols = slice(g * LRU_BLOCK, (g + 1) * LRU_BLOCK)
        steps = range(rb * steps_per_block, (rb + 1) * steps_per_block)
        x_tm = jnp.concatenate(
            [jnp.concatenate([stage_ref[g * slabs + s, pl.ds(step, SUBLANES, stride=PITCH), :]
                              for step in steps], axis=0) for s in range(slabs)], axis=1)
        prev = hist_ref[:, cols] if rb == 0 else lru[g, rb - 1][2]
        lru[g, rb] = _lru_gates(x_tm, prev, p, g)
        if rb == LRU_ROW_BLOCKS - 1:
            hist_ref[:, cols] = lru[g, rb][2]

    def lru_recurrence(g):
        cols = slice(g * LRU_BLOCK, (g + 1) * LRU_BLOCK)
        h = hstate_ref[:, cols]
        for rb in range(LRU_ROW_BLOCKS):
            a, uin, _ = lru[g, rb]
            for i, h in enumerate(_lru_scan(a, uin, h)):
                step = rb * steps_per_block + i
                for s in range(slabs):
                    stage_ref[g * slabs + s, pl.ds(step, SUBLANES, stride=PITCH), :] = (
                        h[:, s * LANES:(s + 1) * LANES])
        hstate_ref[:, cols] = h

    def lru_emit(c0, b):
        pre = _dot(u[b * CHUNK:(b + 1) * CHUNK], wlg_ref[:, c0:c0 + half])
        for r0 in range(0, CHUNK, ROW_CHUNK // 2):
            r1 = r0 + ROW_CHUNK // 2
            hb = jnp.concatenate([stage_ref[(c0 // LANES) + s, b * PITCH + r0:b * PITCH + r1, :]
                                  for s in range(half // LANES)], axis=1)
            zb_ref[b, r0:r1, c0:c0 + half] = (_gelu_tanh(pre[r0:r1]) * hb).astype(bf16)

    items = [(b, hh) for b in range(batch) for hh in range(2)]
    ret = {}

    def ret_project(hp, bp):
        rows = slice(bp * 2 * CHUNK, (bp + 1) * 2 * CHUNK)
        pcols = slice(hp * PAIR, (hp + 1) * PAIR)
        ret[hp, bp] = (_dot(u[rows], wq_ref[:, pcols]), _dot(u[rows], wk_ref[:, pcols]),
                       _dot(u[rows], wv_ref[:, pcols]).astype(bf16),
                       _dot(u[rows], wg_ref[:, pcols]))

    def sub(hp, which, b, hh):
        a = ret[hp, b // 2][which]
        return a[(b % 2) * CHUNK:(b % 2 + 1) * CHUNK, hh * HEAD_DIM:(hh + 1) * HEAD_DIM]

    def ret_scores(hp, it):
        qr = _rot(sub(hp, 0, *it), cq, sq)
        kr = _rot(sub(hp, 1, *it), cq, sq)
        ret[hp, it] = (qr, kr, _dot_t1(qr.astype(bf16), kr.astype(bf16)))

    def ret_outputs(hp, it):
        b, hh = it
        h = 2 * hp + hh
        qr, kr, s = ret[hp, it]
        v = sub(hp, 2, *it)
        state = state_ref[b, h]
        lhs = jnp.concatenate([(s * intra_ref[h]).astype(bf16),
                               (qr * qd_ref[h]).astype(bf16)], axis=1)
        rhs = jnp.concatenate([v, state.astype(bf16)], axis=0)
        ret[hp, it] = _dot(lhs, rhs)
        state_ref[b, h] = (state * chunk_decay[h]
                           + _dot_t0((kr * kd_ref[h]).astype(bf16), v))

    def ret_emit(hp, it):
        b, hh = it
        h = 2 * hp + hh
        o = ret[hp, it]
        on = o * lax.rsqrt(jnp.mean(o * o, axis=-1, keepdims=True) + NORM_EPS)
        g = sub(hp, 3, *it)
        za_ref[b, :, h * HEAD_DIM:(h + 1) * HEAD_DIM] = (
            g * jax.nn.sigmoid(g) * on).astype(bf16)

    def tasks(fn, *fixed, over):
        return [functools.partial(fn, *fixed, k) for k in over]

    zip_tasks(tasks(lru_project, over=range(batch)))
    for hp in range(HEADS // 2):
        zip_tasks(tasks(ret_project, hp, over=range(batch // 2)),
                  tasks(lru_rows, hp, over=range(LRU_ROW_BLOCKS)))
        zip_tasks(tasks(ret_scores, hp, over=items))
        lru_recurrence(hp)
        zip_tasks(tasks(ret_outputs, hp, over=items))
        if hp % 2 == 1:
            zip_tasks(tasks(ret_emit, hp, over=items),
                      tasks(lru_emit, (hp // 2) * half, over=range(batch)))
        else:
            zip_tasks(tasks(ret_emit, hp, over=items))


def _mixer(x, meta_pad, meta_tm, norm_w, w_in, lru_params):
    batch, seq, _ = x.shape
    rot, consts, chunk_decay = _retention_tables(seq)
    tile_spec = pl.BlockSpec((batch, CHUNK, D_MODEL), lambda t: (0, t, 0))
    tab_spec = pl.BlockSpec((CHUNK, HEAD_DIM), lambda t: (t, 0))
    const_spec = _resident((HEADS, CHUNK, HEAD_DIM))
    out = jax.ShapeDtypeStruct((batch, seq, D_MODEL), bf16)
    return pl.pallas_call(
        functools.partial(_mixer_kernel, chunk_decay),
        grid=(seq // CHUNK,),
        in_specs=[tile_spec, _resident(meta_pad.shape), _resident(meta_tm.shape),
                  _resident((1, D_MODEL)),
                  _wcols(COL_Q), _wcols(COL_K), _wcols(COL_V), _wcols(COL_GRET),
                  _wcols(COL_LIN), _wcols(COL_LGATE),
                  tab_spec, tab_spec,
                  _resident((CHUNK, HEAD_DIM)), _resident((CHUNK, HEAD_DIM)),
                  const_spec, const_spec, const_spec]
                 + [_resident(a.shape) for a in lru_params],
        out_specs=[tile_spec, tile_spec],
        out_shape=[out, out],
        scratch_shapes=[pltpu.VMEM((batch, HEADS, HEAD_DIM, HEAD_DIM), f32),
                        pltpu.VMEM((HIST_STEPS * SUBLANES, D_MODEL), f32),
                        pltpu.VMEM((SUBLANES, D_MODEL), f32),
                        pltpu.VMEM((D_MODEL // LANES, batch * PITCH, LANES), f32)],
        compiler_params=pltpu.CompilerParams(
            dimension_semantics=("arbitrary",), vmem_limit_bytes=VMEM_LIMIT_BYTES),
        name="mixer",
    )(x, meta_pad, meta_tm, norm_w, w_in, w_in, w_in, w_in, w_in, w_in, *rot, *consts,
      *lru_params)


def _tail_kernel(x_ref, za_ref, zb_ref, n1_ref, wga_ref, wgb_ref, wr_ref, wl_ref, wo_ref, n2_ref,
                 wfi_ref, wfo_ref, n3_ref, o_ref, act_ref):
    n_sub = x_ref.shape[0] // TAIL_SUB_ROWS
    subs = [slice(i * TAIL_SUB_ROWS, (i + 1) * TAIL_SUB_ROWS) for i in range(n_sub)]
    x = [x_ref[r, :] for r in subs]
    u1 = [_rmsnorm(xi, n1_ref[...]).astype(bf16) for xi in x]
    mixed = [jax.nn.sigmoid(_dot(u1[i], wga_ref[...])) * _dot(za_ref[r, :], wr_ref[...])
             + jax.nn.sigmoid(_dot(u1[i], wgb_ref[...])) * _dot(zb_ref[r, :], wl_ref[...])
             for i, r in enumerate(subs)]
    h = [x[i] + _dot(mixed[i].astype(bf16), wo_ref[...]) for i in range(n_sub)]
    u = [_rmsnorm(hi, n2_ref[...]).astype(bf16) for hi in h]
    for j in range(FFN_HIDDEN // FFN_COLS):
        for i, r in enumerate(subs):
            g = _dot(u[i], wfi_ref[:, j * FFN_COLS:(j + 1) * FFN_COLS])
            up = _dot(u[i], wfi_ref[:, FFN_HIDDEN + j * FFN_COLS:FFN_HIDDEN + (j + 1) * FFN_COLS])
            act_ref[r, j * FFN_COLS:(j + 1) * FFN_COLS] = (
                g * jax.nn.sigmoid(g) * up).astype(bf16)
    for i, r in enumerate(subs):
        o_ref[r, :] = _rmsnorm(h[i] + _dot(act_ref[r, :], wfo_ref[...]), n3_ref[...])


def _tail(x2d, za, zb, norm1, w_in, weights):
    n = x2d.shape[0]
    row_spec = pl.BlockSpec((TAIL_ROWS, D_MODEL), lambda i: (i, 0))
    return pl.pallas_call(
        _tail_kernel,
        grid=(n // TAIL_ROWS,),
        in_specs=[row_spec, row_spec, row_spec, _resident((1, D_MODEL)),
                  _wcols(COL_GA), _wcols(COL_GB)] + [_resident(w.shape) for w in weights],
        out_specs=row_spec,
        out_shape=jax.ShapeDtypeStruct((n, D_MODEL), f32),
        scratch_shapes=[pltpu.VMEM((TAIL_ROWS, FFN_HIDDEN), bf16)],
        compiler_params=pltpu.CompilerParams(
            dimension_semantics=("parallel",), vmem_limit_bytes=VMEM_LIMIT_BYTES),
        name="merge_ffn",
    )(x2d, za, zb, norm1, w_in, w_in, *weights)


def kernel(x, meta_tokens, mix_norm_w, w_in, conv_w, conv_b, lru_wa, lru_ba, lru_wx, lru_bx,
           lru_lambda, w_branch_ret, w_branch_lru, w_out, ffn_norm_w, w_ffn_in, w_ffn_out,
           final_norm_w):
    batch, seq, d = x.shape
    assert d == D_MODEL and w_in.shape == (1, D_MODEL, 8 * D_MODEL), "single-layer block only"
    assert meta_tokens.shape == (N_META, D_MODEL)
    assert batch == SUBLANES and seq % CHUNK == 0 and (batch * seq) % TAIL_ROWS == 0

    row = lambda a: a.reshape(1, -1).astype(f32)
    meta = meta_tokens.astype(f32)
    meta_pad = jnp.concatenate([jnp.zeros((CHUNK - N_META, D_MODEL), f32), meta], axis=0)
    meta_tm = jnp.repeat(meta, SUBLANES, axis=0)
    w_in_b = w_in[0].astype(bf16)
    norm1 = row(mix_norm_w[0])

    wax = jnp.concatenate([lru_wa[0], lru_wx[0]], axis=-1).astype(bf16)
    bax = jnp.concatenate([lru_ba[0].reshape(LRU_BLOCKS, 1, LRU_BLOCK),
                           lru_bx[0].reshape(LRU_BLOCKS, 1, LRU_BLOCK)], axis=-1).astype(f32)
    lru_params = (conv_w[0].astype(f32), row(conv_b[0]), wax, bax, row(lru_lambda[0]))
    za, zb = _mixer(x, meta_pad, meta_tm, norm1, w_in_b, lru_params)

    weights = (w_branch_ret[0].astype(bf16), w_branch_lru[0].astype(bf16), w_out[0].astype(bf16),
               row(ffn_norm_w[0]), w_ffn_in[0].astype(bf16), w_ffn_out[0].astype(bf16),
               row(final_norm_w))
    n = batch * seq
    out = _tail(x.reshape(n, D_MODEL), za.reshape(n, D_MODEL), zb.reshape(n, D_MODEL),
                norm1, w_in_b, weights)
    return out.reshape(batch, seq, D_MODEL)
```

```python
import functools

import numpy as np
import jax
import jax.numpy as jnp
from jax import lax
from jax.experimental import pallas as pl
from jax.experimental.pallas import tpu as pltpu

D_MODEL = 1024
N_META = 16
HEADS = 8
HEAD_DIM = 128
CHUNK = 128
ROPE_BASE = 10000.0
LRU_BLOCKS = 4
LRU_BLOCK = D_MODEL // LRU_BLOCKS
CONV_WIDTH = 4
LRU_C = 8.0
FFN_HIDDEN = 2816
NORM_EPS = 1e-6

COL_Q, COL_K, COL_V, COL_GRET, COL_LIN, COL_LGATE, COL_GA, COL_GB = range(8)

SUBLANES = 8
LANES = 128
VMEM_LIMIT_BYTES = 60 * 1024 * 1024
TAIL_ROWS = 512
TAIL_SUB_ROWS = 256
FFN_COLS = 256
PITCH = CHUNK + SUBLANES
HIST_STEPS = CONV_WIDTH - 1
LRU_ROW_BLOCKS = 4
ROW_CHUNK = 64
PAIR = 2 * HEAD_DIM

f32 = jnp.float32
bf16 = jnp.bfloat16


def _rmsnorm(x, w):
    return x * lax.rsqrt(jnp.mean(x * x, axis=-1, keepdims=True) + NORM_EPS) * w


def _dot(a, b):
    return jnp.dot(a, b, preferred_element_type=f32)


def _resident(shape):
    nd = len(shape)
    return pl.BlockSpec(shape, lambda *_: (0,) * nd, pipeline_mode=pl.Buffered(1))


def _wcols(col):
    return pl.BlockSpec((D_MODEL, D_MODEL), lambda *_: (0, col), pipeline_mode=pl.Buffered(1))


def _rot(x, cos, sin_signed):
    return x * cos + pltpu.roll(x, HEAD_DIM // 2, axis=1) * sin_signed


def _dot_t0(a, b):
    return lax.dot_general(a, b, (((0,), (0,)), ((), ())), preferred_element_type=f32)


def _dot_t1(a, b):
    return lax.dot_general(a, b, (((1,), (1,)), ((), ())), preferred_element_type=f32)


def _retention_tables(seq):
    inv_freq = ROPE_BASE ** (-np.arange(0, HEAD_DIM, 2, dtype=np.float64) / HEAD_DIM)

    def tables(pos):
        ang = pos.astype(np.float64)[:, None] * inv_freq[None, :]
        cos = np.cos(ang)
        sin = np.sin(ang)
        return (jnp.asarray(np.concatenate([cos, cos], axis=1), f32),
                jnp.asarray(np.concatenate([-sin, sin], axis=1), f32))

    cq, sq = tables(np.arange(seq) + N_META)
    cm, sm = tables(np.maximum(np.arange(CHUNK) - (CHUNK - N_META), 0))
    scale = HEAD_DIM ** -0.5
    log_g = np.log(1.0 - 2.0 ** (-5.0 - np.arange(HEADS, dtype=np.float64)))
    idx = np.arange(CHUNK, dtype=np.float64)
    diff = idx[:, None] - idx[None, :]
    intra = np.where(diff[None] >= 0, np.exp(np.maximum(diff, 0.0)[None] * log_g[:, None, None]), 0.0)
    ones = np.ones((1, 1, HEAD_DIM))
    qd = np.exp((idx + 1.0)[None, :, None] * log_g[:, None, None]) * ones
    kd = np.exp((CHUNK - 1.0 - idx)[None, :, None] * log_g[:, None, None]) * ones
    chunk_decay = tuple(float(v) for v in np.exp(CHUNK * log_g))
    consts = tuple(jnp.asarray(a, f32) for a in (intra * scale, qd, kd * scale))
    return (cq, sq, cm, sm), consts, chunk_decay


def _gelu_tanh(x):
    k0 = -2.0 * 0.7978845608028654
    k1 = -2.0 * 0.035677408136300125
    return x / (1.0 + jnp.exp(x * (k0 + k1 * (x * x))))


def _lru_gates(x_tm, prev, p, g):
    cw_ref, cb_ref, wax_ref, bax_ref, lam_ref = p
    cols = slice(g * LRU_BLOCK, (g + 1) * LRU_BLOCK)
    n = x_tm.shape[0]
    x_ext = jnp.concatenate([prev, x_tm], axis=0)
    chunks = [slice(r0, r0 + ROW_CHUNK) for r0 in range(0, n, ROW_CHUNK)]
    c = []
    for rows in chunks:
        cc = cb_ref[:, cols] + cw_ref[CONV_WIDTH - 1:CONV_WIDTH, cols] * x_tm[rows]
        for j in range(CONV_WIDTH - 1):
            cc = cc + cw_ref[j:j + 1, cols] * x_ext[j * SUBLANES + rows.start:
                                                    j * SUBLANES + rows.stop]
        c.append(cc)
    pre = _dot(jnp.concatenate(c, axis=0).astype(bf16), wax_ref[g])
    z = -lam_ref[:, cols]
    decay = LRU_C * (jnp.maximum(z, 0.0) + jnp.log1p(jnp.exp(-jnp.abs(z))))
    a_parts, u_parts = [], []
    for cc, rows in zip(c, chunks):
        ri = jax.nn.sigmoid(pre[rows] + bax_ref[g])
        r = ri[:, :LRU_BLOCK]
        i = ri[:, LRU_BLOCK:]
        neg_log_a = r * decay
        a = jnp.exp(-neg_log_a)
        y = jnp.tanh(neg_log_a) * (a * a + 1.0)
        mult = jnp.where(y > 0.0, y * lax.rsqrt(y), 0.0)
        a_parts.append(a)
        u_parts.append(mult * (i * cc))
    return jnp.concatenate(a_parts, axis=0), jnp.concatenate(u_parts, axis=0), x_ext[n:]


def _lru_scan(a, u, h):
    states = []
    for step in range(a.shape[0] // SUBLANES):
        rows = slice(step * SUBLANES, (step + 1) * SUBLANES)
        h = a[rows] * h + u[rows]
        states.append(h)
    return states


def _mixer_kernel(chunk_decay, x_ref, metap_ref, metat_ref, nw_ref,
                  wq_ref, wk_ref, wv_ref, wg_ref, wl_ref, wlg_ref,
                  cq_ref, sq_ref, cm_ref, sm_ref, intra_ref, qd_ref, kd_ref,
                  cw_ref, cb_ref, wax_ref, bax_ref, lam_ref,
                  za_ref, zb_ref, state_ref, hist_ref, hstate_ref, stage_ref):
    batch = x_ref.shape[0]
    assert batch == SUBLANES
    t = pl.program_id(0)
    p = (cw_ref, cb_ref, wax_ref, bax_ref, lam_ref)
    slabs = LRU_BLOCK // LANES
    half = D_MODEL // 2

    @pl.when(t == 0)
    def _():
        um = _rmsnorm(metap_ref[...], nw_ref[...]).astype(bf16)
        km = _dot(um, wk_ref[...])
        vm = _dot(um, wv_ref[...]).astype(bf16)
        for h in range(HEADS):
            cols = slice(h * HEAD_DIM, (h + 1) * HEAD_DIM)
            kr = _rot(km[:, cols], cm_ref[...], sm_ref[...])
            st = _dot_t0((kr * kd_ref[h]).astype(bf16), vm[:, cols])
            for b in range(batch):
                state_ref[b, h] = st
        lin = _dot(_rmsnorm(metat_ref[...], nw_ref[...]).astype(bf16), wl_ref[...])
        for g in range(LRU_BLOCKS):
            cols = slice(g * LRU_BLOCK, (g + 1) * LRU_BLOCK)
            a, uin, hist = _lru_gates(lin[:, cols],
                                      jnp.zeros((HIST_STEPS * SUBLANES, LRU_BLOCK), f32), p, g)
            hist_ref[:, cols] = hist
            hstate_ref[:, cols] = _lru_scan(a, uin, jnp.zeros((SUBLANES, LRU_BLOCK), f32))[-1]

    u = _rmsnorm(x_ref[...].reshape(batch * CHUNK, D_MODEL), nw_ref[...]).astype(bf16)
    cq = cq_ref[...]
    sq = sq_ref[...]

    def zip_tasks(*task_lists):
        n = max(len(tl) for tl in task_lists)
        for k in range(n):
            for tl in task_lists:
                for task in tl[k * len(tl) // n:(k + 1) * len(tl) // n]:
                    task()

    def lru_project(b):
        lin = _dot(u[b * CHUNK:(b + 1) * CHUNK], wl_ref[...])
        for s in range(D_MODEL // LANES):
            stage_ref[s, b * PITCH:b * PITCH + CHUNK, :] = lin[:, s * LANES:(s + 1) * LANES]

    lru = {}
    steps_per_block = CHUNK // LRU_ROW_BLOCKS

    def lru_rows(g, rb):
        cols = slice(g * LRU_BLOCK, (g + 1) * LRU_BLOCK)
        steps = range(rb * steps_per_block, (rb + 1) * steps_per_block)
        x_tm = jnp.concatenate(
            [jnp.concatenate([stage_ref[g * slabs + s, pl.ds(step, SUBLANES, stride=PITCH), :]
                              for step in steps], axis=0) for s in range(slabs)], axis=1)
        prev = hist_ref[:, cols] if rb == 0 else lru[g, rb - 1][2]
        lru[g, rb] = _lru_gates(x_tm, prev, p, g)
        if rb == LRU_ROW_BLOCKS - 1:
            hist_ref[:, cols] = lru[g, rb][2]

    def lru_recurrence(g):
        cols = slice(g * LRU_BLOCK, (g + 1) * LRU_BLOCK)
        h = hstate_ref[:, cols]
        for rb in range(LRU_ROW_BLOCKS):
            a, uin, _ = lru[g, rb]
            for i, h in enumerate(_lru_scan(a, uin, h)):
                step = rb * steps_per_block + i
                for s in range(slabs):
                    stage_ref[g * slabs + s, pl.ds(step, SUBLANES, stride=PITCH), :] = (
                        h[:, s * LANES:(s + 1) * LANES])
        hstate_ref[:, cols] = h

    def lru_emit(c0, b):
        pre = _dot(u[b * CHUNK:(b + 1) * CHUNK], wlg_ref[:, c0:c0 + half])
        for r0 in range(0, CHUNK, ROW_CHUNK // 2):
            r1 = r0 + ROW_CHUNK // 2
            hb = jnp.concatenate([stage_ref[(c0 // LANES) + s, b * PITCH + r0:b * PITCH + r1, :]
                                  for s in range(half // LANES)], axis=1)
            zb_ref[b, r0:r1, c0:c0 + half] = (_gelu_tanh(pre[r0:r1]) * hb).astype(bf16)

    items = [(b, hh) for b in range(batch) for hh in range(2)]
    ret = {}

    def ret_project(hp, bp):
        rows = slice(bp * 2 * CHUNK, (bp + 1) * 2 * CHUNK)
        pcols = slice(hp * PAIR, (hp + 1) * PAIR)
        ret[hp, bp] = (_dot(u[rows], wq_ref[:, pcols]), _dot(u[rows], wk_ref[:, pcols]),
                       _dot(u[rows], wv_ref[:, pcols]).astype(bf16),
                       _dot(u[rows], wg_ref[:, pcols]))

    def sub(hp, which, b, hh):
        a = ret[hp, b // 2][which]
        return a[(b % 2) * CHUNK:(b % 2 + 1) * CHUNK, hh * HEAD_DIM:(hh + 1) * HEAD_DIM]

    def ret_scores(hp, it):
        qr = _rot(sub(hp, 0, *it), cq, sq)
        kr = _rot(sub(hp, 1, *it), cq, sq)
        ret[hp, it] = (qr, kr, _dot_t1(qr.astype(bf16), kr.astype(bf16)))

    def ret_outputs(hp, it):
        b, hh = it
        h = 2 * hp + hh
        qr, kr, s = ret[hp, it]
        v = sub(hp, 2, *it)
        state = state_ref[b, h]
        lhs = jnp.concatenate([(s * intra_ref[h]).astype(bf16),
                               (qr * qd_ref[h]).astype(bf16)], axis=1)
        rhs = jnp.concatenate([v, state.astype(bf16)], axis=0)
        ret[hp, it] = _dot(lhs, rhs)
        state_ref[b, h] = (state * chunk_decay[h]
                           + _dot_t0((kr * kd_ref[h]).astype(bf16), v))

    def ret_emit(hp, it):
        b, hh = it
        h = 2 * hp + hh
        o = ret[hp, it]
        on = o * lax.rsqrt(jnp.mean(o * o, axis=-1, keepdims=True) + NORM_EPS)
        g = sub(hp, 3, *it)
        za_ref[b, :, h * HEAD_DIM:(h + 1) * HEAD_DIM] = (
            g * jax.nn.sigmoid(g) * on).astype(bf16)

    def tasks(fn, *fixed, over):
        return [functools.partial(fn, *fixed, k) for k in over]

    zip_tasks(tasks(lru_project, over=range(batch)))
    for hp in range(HEADS // 2):
        zip_tasks(tasks(ret_project, hp, over=range(batch // 2)),
                  tasks(lru_rows, hp, over=range(LRU_ROW_BLOCKS)))
        emit = tasks(lru_emit, (hp // 2) * half, over=range(batch)) if hp % 2 == 1 else []
        for part in range(2):
            group = items[part * len(items) // 2:(part + 1) * len(items) // 2]
            zip_tasks(tasks(ret_scores, hp, over=group))
            if part == 0:
                lru_recurrence(hp)
            zip_tasks(tasks(ret_outputs, hp, over=group))
            zip_tasks(tasks(ret_emit, hp, over=group),
                      emit[part * len(emit) // 2:(part + 1) * len(emit) // 2])


def _mixer(x, meta_pad, meta_tm, norm_w, w_in, lru_params):
    batch, seq, _ = x.shape
    rot, consts, chunk_decay = _retention_tables(seq)
    tile_spec = pl.BlockSpec((batch, CHUNK, D_MODEL), lambda t: (0, t, 0))
    tab_spec = pl.BlockSpec((CHUNK, HEAD_DIM), lambda t: (t, 0))
    const_spec = _resident((HEADS, CHUNK, HEAD_DIM))
    out = jax.ShapeDtypeStruct((batch, seq, D_MODEL), bf16)
    return pl.pallas_call(
        functools.partial(_mixer_kernel, chunk_decay),
        grid=(seq // CHUNK,),
        in_specs=[tile_spec, _resident(meta_pad.shape), _resident(meta_tm.shape),
                  _resident((1, D_MODEL)),
                  _wcols(COL_Q), _wcols(COL_K), _wcols(COL_V), _wcols(COL_GRET),
                  _wcols(COL_LIN), _wcols(COL_LGATE),
                  tab_spec, tab_spec,
                  _resident((CHUNK, HEAD_DIM)), _resident((CHUNK, HEAD_DIM)),
                  const_spec, const_spec, const_spec]
                 + [_resident(a.shape) for a in lru_params],
        out_specs=[tile_spec, tile_spec],
        out_shape=[out, out],
        scratch_shapes=[pltpu.VMEM((batch, HEADS, HEAD_DIM, HEAD_DIM), f32),
                        pltpu.VMEM((HIST_STEPS * SUBLANES, D_MODEL), f32),
                        pltpu.VMEM((SUBLANES, D_MODEL), f32),
                        pltpu.VMEM((D_MODEL // LANES, batch * PITCH, LANES), f32)],
        compiler_params=pltpu.CompilerParams(
            dimension_semantics=("arbitrary",), vmem_limit_bytes=VMEM_LIMIT_BYTES),
        name="mixer",
    )(x, meta_pad, meta_tm, norm_w, w_in, w_in, w_in, w_in, w_in, w_in, *rot, *consts,
      *lru_params)


def _tail_kernel(x_ref, za_ref, zb_ref, n1_ref, wga_ref, wgb_ref, wr_ref, wl_ref, wo_ref, n2_ref,
                 wfi_ref, wfo_ref, n3_ref, o_ref, act_ref):
    n_sub = x_ref.shape[0] // TAIL_SUB_ROWS
    subs = [slice(i * TAIL_SUB_ROWS, (i + 1) * TAIL_SUB_ROWS) for i in range(n_sub)]
    x = [x_ref[r, :] for r in subs]
    u1 = [_rmsnorm(xi, n1_ref[...]).astype(bf16) for xi in x]
    mixed = [jax.nn.sigmoid(_dot(u1[i], wga_ref[...])) * _dot(za_ref[r, :], wr_ref[...])
             + jax.nn.sigmoid(_dot(u1[i], wgb_ref[...])) * _dot(zb_ref[r, :], wl_ref[...])
             for i, r in enumerate(subs)]
    h = [x[i] + _dot(mixed[i].astype(bf16), wo_ref[...]) for i in range(n_sub)]
    u = [_rmsnorm(hi, n2_ref[...]).astype(bf16) for hi in h]
    for j in range(FFN_HIDDEN // FFN_COLS):
        for i, r in enumerate(subs):
            g = _dot(u[i], wfi_ref[:, j * FFN_COLS:(j + 1) * FFN_COLS])
            up = _dot(u[i], wfi_ref[:, FFN_HIDDEN + j * FFN_COLS:FFN_HIDDEN + (j + 1) * FFN_COLS])
            act_ref[r, j * FFN_COLS:(j + 1) * FFN_COLS] = (
                g * jax.nn.sigmoid(g) * up).astype(bf16)
    for i, r in enumerate(subs):
        o_ref[r, :] = _rmsnorm(h[i] + _dot(act_ref[r, :], wfo_ref[...]), n3_ref[...])


def _tail(x2d, za, zb, norm1, w_in, weights):
    n = x2d.shape[0]
    row_spec = pl.BlockSpec((TAIL_ROWS, D_MODEL), lambda i: (i, 0))
    return pl.pallas_call(
        _tail_kernel,
        grid=(n // TAIL_ROWS,),
        in_specs=[row_spec, row_spec, row_spec, _resident((1, D_MODEL)),
                  _wcols(COL_GA), _wcols(COL_GB)] + [_resident(w.shape) for w in weights],
        out_specs=row_spec,
        out_shape=jax.ShapeDtypeStruct((n, D_MODEL), f32),
        scratch_shapes=[pltpu.VMEM((TAIL_ROWS, FFN_HIDDEN), bf16)],
        compiler_params=pltpu.CompilerParams(
            dimension_semantics=("parallel",), vmem_limit_bytes=VMEM_LIMIT_BYTES),
        name="merge_ffn",
    )(x2d, za, zb, norm1, w_in, w_in, *weights)


def kernel(x, meta_tokens, mix_norm_w, w_in, conv_w, conv_b, lru_wa, lru_ba, lru_wx, lru_bx,
           lru_lambda, w_branch_ret, w_branch_lru, w_out, ffn_norm_w, w_ffn_in, w_ffn_out,
           final_norm_w):
    batch, seq, d = x.shape
    assert d == D_MODEL and w_in.shape == (1, D_MODEL, 8 * D_MODEL), "single-layer block only"
    assert meta_tokens.shape == (N_META, D_MODEL)
    assert batch == SUBLANES and seq % CHUNK == 0 and (batch * seq) % TAIL_ROWS == 0

    row = lambda a: a.reshape(1, -1).astype(f32)
    meta = meta_tokens.astype(f32)
    meta_pad = jnp.concatenate([jnp.zeros((CHUNK - N_META, D_MODEL), f32), meta], axis=0)
    meta_tm = jnp.repeat(meta, SUBLANES, axis=0)
    w_in_b = w_in[0].astype(bf16)
    norm1 = row(mix_norm_w[0])

    wax = jnp.concatenate([lru_wa[0], lru_wx[0]], axis=-1).astype(bf16)
    bax = jnp.concatenate([lru_ba[0].reshape(LRU_BLOCKS, 1, LRU_BLOCK),
                           lru_bx[0].reshape(LRU_BLOCKS, 1, LRU_BLOCK)], axis=-1).astype(f32)
    lru_params = (conv_w[0].astype(f32), row(conv_b[0]), wax, bax, row(lru_lambda[0]))
    za, zb = _mixer(x, meta_pad, meta_tm, norm1, w_in_b, lru_params)

    weights = (w_branch_ret[0].astype(bf16), w_branch_lru[0].astype(bf16), w_out[0].astype(bf16),
               row(ffn_norm_w[0]), w_ffn_in[0].astype(bf16), w_ffn_out[0].astype(bf16),
               row(final_norm_w))
    n = batch * seq
    out = _tail(x.reshape(n, D_MODEL), za.reshape(n, D_MODEL), zb.reshape(n, D_MODEL),
                norm1, w_in_b, weights)
    return out.reshape(batch, seq, D_MODEL)
```

```python
import functools

import numpy as np
import jax
import jax.numpy as jnp
from jax import lax
from jax.experimental import pallas as pl
from jax.experimental.pallas import tpu as pltpu

D_MODEL = 1024
N_META = 16
HEADS = 8
HEAD_DIM = 128
CHUNK = 128
ROPE_BASE = 10000.0
LRU_BLOCKS = 4
LRU_BLOCK = D_MODEL // LRU_BLOCKS
CONV_WIDTH = 4
LRU_C = 8.0
FFN_HIDDEN = 2816
NORM_EPS = 1e-6

COL_Q, COL_K, COL_V, COL_GRET, COL_LIN, COL_LGATE, COL_GA, COL_GB = range(8)

SUBLANES = 8
LANES = 128
VMEM_LIMIT_BYTES = 60 * 1024 * 1024
TAIL_ROWS = 512
TAIL_SUB_ROWS = 256
FFN_COLS = 256
PITCH = CHUNK + SUBLANES
HIST_STEPS = CONV_WIDTH - 1
LRU_ROW_BLOCKS = 4
ROW_CHUNK = 64
PAIR = 2 * HEAD_DIM

f32 = jnp.float32
bf16 = jnp.bfloat16


def _rmsnorm(x, w):
    return x * lax.rsqrt(jnp.mean(x * x, axis=-1, keepdims=True) + NORM_EPS) * w


def _dot(a, b):
    return jnp.dot(a, b, preferred_element_type=f32)


def _resident(shape):
    nd = len(shape)
    return pl.BlockSpec(shape, lambda *_: (0,) * nd, pipeline_mode=pl.Buffered(1))


def _wcols(col):
    return pl.BlockSpec((D_MODEL, D_MODEL), lambda *_: (0, col), pipeline_mode=pl.Buffered(1))


def _rot(x, cos, sin_signed):
    return x * cos + pltpu.roll(x, HEAD_DIM // 2, axis=1) * sin_signed


def _dot_t0(a, b):
    return lax.dot_general(a, b, (((0,), (0,)), ((), ())), preferred_element_type=f32)


def _dot_t1(a, b):
    return lax.dot_general(a, b, (((1,), (1,)), ((), ())), preferred_element_type=f32)


def _retention_tables(seq):
    inv_freq = ROPE_BASE ** (-np.arange(0, HEAD_DIM, 2, dtype=np.float64) / HEAD_DIM)

    def tables(pos):
        ang = pos.astype(np.float64)[:, None] * inv_freq[None, :]
        cos = np.cos(ang)
        sin = np.sin(ang)
        return (jnp.asarray(np.concatenate([cos, cos], axis=1), f32),
                jnp.asarray(np.concatenate([-sin, sin], axis=1), f32))

    cq, sq = tables(np.arange(seq) + N_META)
    cm, sm = tables(np.maximum(np.arange(CHUNK) - (CHUNK - N_META), 0))
    scale = HEAD_DIM ** -0.5
    log_g = np.log(1.0 - 2.0 ** (-5.0 - np.arange(HEADS, dtype=np.float64)))
    idx = np.arange(CHUNK, dtype=np.float64)
    diff = idx[:, None] - idx[None, :]
    intra = np.where(diff[None] >= 0, np.exp(np.maximum(diff, 0.0)[None] * log_g[:, None, None]), 0.0)
    ones = np.ones((1, 1, HEAD_DIM))
    qd = np.exp((idx + 1.0)[None, :, None] * log_g[:, None, None]) * ones
    kd = np.exp((CHUNK - 1.0 - idx)[None, :, None] * log_g[:, None, None]) * ones
    chunk_decay = tuple(float(v) for v in np.exp(CHUNK * log_g))
    consts = tuple(jnp.asarray(a, f32) for a in (intra * scale, qd, kd * scale))
    return (cq, sq, cm, sm), consts, chunk_decay


def _gelu_tanh(x):
    k0 = -2.0 * 0.7978845608028654
    k1 = -2.0 * 0.035677408136300125
    return x / (1.0 + jnp.exp(x * (k0 + k1 * (x * x))))


def _lru_gates(x_tm, prev, p, g):
    cw_ref, cb_ref, wax_ref, bax_ref, lam_ref = p
    cols = slice(g * LRU_BLOCK, (g + 1) * LRU_BLOCK)
    n = x_tm.shape[0]
    x_ext = jnp.concatenate([prev, x_tm], axis=0)
    chunks = [slice(r0, r0 + ROW_CHUNK) for r0 in range(0, n, ROW_CHUNK)]
    c = []
    for rows in chunks:
        cc = cb_ref[:, cols] + cw_ref[CONV_WIDTH - 1:CONV_WIDTH, cols] * x_tm[rows]
        for j in range(CONV_WIDTH - 1):
            cc = cc + cw_ref[j:j + 1, cols] * x_ext[j * SUBLANES + rows.start:
                                                    j * SUBLANES + rows.stop]
        c.append(cc)
    pre = _dot(jnp.concatenate(c, axis=0).astype(bf16), wax_ref[g])
    z = -lam_ref[:, cols]
    decay = LRU_C * (jnp.maximum(z, 0.0) + jnp.log1p(jnp.exp(-jnp.abs(z))))
    a_parts, u_parts = [], []
    for cc, rows in zip(c, chunks):
        ri = jax.nn.sigmoid(pre[rows] + bax_ref[g])
        r = ri[:, :LRU_BLOCK]
        i = ri[:, LRU_BLOCK:]
        neg_log_a = r * decay
        a = jnp.exp(-neg_log_a)
        y = jnp.tanh(neg_log_a) * (a * a + 1.0)
        mult = jnp.where(y > 0.0, y * lax.rsqrt(y), 0.0)
        a_parts.append(a)
        u_parts.append(mult * (i * cc))
    return jnp.concatenate(a_parts, axis=0), jnp.concatenate(u_parts, axis=0), x_ext[n:]


def _lru_scan(a, u, h):
    states = []
    for step in range(a.shape[0] // SUBLANES):
        rows = slice(step * SUBLANES, (step + 1) * SUBLANES)
        h = a[rows] * h + u[rows]
        states.append(h)
    return states


def _mixer_kernel(chunk_decay, x_ref, metap_ref, metat_ref, nw_ref,
                  wq_ref, wk_ref, wv_ref, wg_ref, wl_ref,
                  cq_ref, sq_ref, cm_ref, sm_ref, intra_ref, qd_ref, kd_ref,
                  cw_ref, cb_ref, wax_ref, bax_ref, lam_ref,
                  za_ref, zb_ref, state_ref, hist_ref, hstate_ref, stage_ref):
    batch = x_ref.shape[0]
    assert batch == SUBLANES
    t = pl.program_id(0)
    p = (cw_ref, cb_ref, wax_ref, bax_ref, lam_ref)
    slabs = LRU_BLOCK // LANES
    half = D_MODEL // 2

    @pl.when(t == 0)
    def _():
        um = _rmsnorm(metap_ref[...], nw_ref[...]).astype(bf16)
        km = _dot(um, wk_ref[...])
        vm = _dot(um, wv_ref[...]).astype(bf16)
        for h in range(HEADS):
            cols = slice(h * HEAD_DIM, (h + 1) * HEAD_DIM)
            kr = _rot(km[:, cols], cm_ref[...], sm_ref[...])
            st = _dot_t0((kr * kd_ref[h]).astype(bf16), vm[:, cols])
            for b in range(batch):
                state_ref[b, h] = st
        lin = _dot(_rmsnorm(metat_ref[...], nw_ref[...]).astype(bf16), wl_ref[...])
        for g in range(LRU_BLOCKS):
            cols = slice(g * LRU_BLOCK, (g + 1) * LRU_BLOCK)
            a, uin, hist = _lru_gates(lin[:, cols],
                                      jnp.zeros((HIST_STEPS * SUBLANES, LRU_BLOCK), f32), p, g)
            hist_ref[:, cols] = hist
            hstate_ref[:, cols] = _lru_scan(a, uin, jnp.zeros((SUBLANES, LRU_BLOCK), f32))[-1]

    u = _rmsnorm(x_ref[...].reshape(batch * CHUNK, D_MODEL), nw_ref[...]).astype(bf16)
    cq = cq_ref[...]
    sq = sq_ref[...]

    def zip_tasks(*task_lists):
        n = max(len(tl) for tl in task_lists)
        for k in range(n):
            for tl in task_lists:
                for task in tl[k * len(tl) // n:(k + 1) * len(tl) // n]:
                    task()

    def lru_project(b):
        lin = _dot(u[b * CHUNK:(b + 1) * CHUNK], wl_ref[...])
        for s in range(D_MODEL // LANES):
            stage_ref[s, b * PITCH:b * PITCH + CHUNK, :] = lin[:, s * LANES:(s + 1) * LANES]

    lru = {}
    steps_per_block = CHUNK // LRU_ROW_BLOCKS

    def lru_rows(g, rb):
        cols = slice(g * LRU_BLOCK, (g + 1) * LRU_BLOCK)
        steps = range(rb * steps_per_block, (rb + 1) * steps_per_block)
        x_tm = jnp.concatenate(
            [jnp.concatenate([stage_ref[g * slabs + s, pl.ds(step, SUBLANES, stride=PITCH), :]
                              for step in steps], axis=0) for s in range(slabs)], axis=1)
        prev = hist_ref[:, cols] if rb == 0 else lru[g, rb - 1][2]
        lru[g, rb] = _lru_gates(x_tm, prev, p, g)
        if rb == LRU_ROW_BLOCKS - 1:
            hist_ref[:, cols] = lru[g, rb][2]

    def lru_recurrence(g):
        cols = slice(g * LRU_BLOCK, (g + 1) * LRU_BLOCK)
        h = hstate_ref[:, cols]
        for rb in range(LRU_ROW_BLOCKS):
            a, uin, _ = lru[g, rb]
            for i, h in enumerate(_lru_scan(a, uin, h)):
                step = rb * steps_per_block + i
                for s in range(slabs):
                    stage_ref[g * slabs + s, pl.ds(step, SUBLANES, stride=PITCH), :] = (
                        h[:, s * LANES:(s + 1) * LANES])
        hstate_ref[:, cols] = h

    def lru_emit(c0, b):
        hb = jnp.concatenate([stage_ref[(c0 // LANES) + s, b * PITCH:b * PITCH + CHUNK, :]
                              for s in range(half // LANES)], axis=1)
        zb_ref[b, :, c0:c0 + half] = hb.astype(bf16)

    items = [(b, hh) for b in range(batch) for hh in range(2)]
    ret = {}

    def ret_project(hp, bp):
        rows = slice(bp * 2 * CHUNK, (bp + 1) * 2 * CHUNK)
        pcols = slice(hp * PAIR, (hp + 1) * PAIR)
        ret[hp, bp] = (_dot(u[rows], wq_ref[:, pcols]), _dot(u[rows], wk_ref[:, pcols]),
                       _dot(u[rows], wv_ref[:, pcols]).astype(bf16),
                       _dot(u[rows], wg_ref[:, pcols]))

    def sub(hp, which, b, hh):
        a = ret[hp, b // 2][which]
        return a[(b % 2) * CHUNK:(b % 2 + 1) * CHUNK, hh * HEAD_DIM:(hh + 1) * HEAD_DIM]

    def ret_scores(hp, it):
        qr = _rot(sub(hp, 0, *it), cq, sq)
        kr = _rot(sub(hp, 1, *it), cq, sq)
        ret[hp, it] = (qr, kr, _dot_t1(qr.astype(bf16), kr.astype(bf16)))

    def ret_outputs(hp, it):
        b, hh = it
        h = 2 * hp + hh
        qr, kr, s = ret[hp, it]
        v = sub(hp, 2, *it)
        state = state_ref[b, h]
        lhs = jnp.concatenate([(s * intra_ref[h]).astype(bf16),
                               (qr * qd_ref[h]).astype(bf16)], axis=1)
        rhs = jnp.concatenate([v, state.astype(bf16)], axis=0)
        ret[hp, it] = _dot(lhs, rhs)
        state_ref[b, h] = (state * chunk_decay[h]
                           + _dot_t0((kr * kd_ref[h]).astype(bf16), v))

    def ret_emit(hp, it):
        b, hh = it
        h = 2 * hp + hh
        o = ret[hp, it]
        on = o * lax.rsqrt(jnp.mean(o * o, axis=-1, keepdims=True) + NORM_EPS)
        g = sub(hp, 3, *it)
        za_ref[b, :, h * HEAD_DIM:(h + 1) * HEAD_DIM] = (
            g * jax.nn.sigmoid(g) * on).astype(bf16)

    def tasks(fn, *fixed, over):
        return [functools.partial(fn, *fixed, k) for k in over]

    zip_tasks(tasks(lru_project, over=range(batch)))
    for hp in range(HEADS // 2):
        zip_tasks(tasks(ret_project, hp, over=range(batch // 2)),
                  tasks(lru_rows, hp, over=range(LRU_ROW_BLOCKS)))
        emit = tasks(lru_emit, (hp // 2) * half, over=range(batch)) if hp % 2 == 1 else []
        for part in range(2):
            group = items[part * len(items) // 2:(part + 1) * len(items) // 2]
            zip_tasks(tasks(ret_scores, hp, over=group))
            if part == 0:
                lru_recurrence(hp)
            zip_tasks(tasks(ret_outputs, hp, over=group))
            zip_tasks(tasks(ret_emit, hp, over=group),
                      emit[part * len(emit) // 2:(part + 1) * len(emit) // 2])


def _mixer(x, meta_pad, meta_tm, norm_w, w_in, lru_params):
    batch, seq, _ = x.shape
    rot, consts, chunk_decay = _retention_tables(seq)
    tile_spec = pl.BlockSpec((batch, CHUNK, D_MODEL), lambda t: (0, t, 0))
    tab_spec = pl.BlockSpec((CHUNK, HEAD_DIM), lambda t: (t, 0))
    const_spec = _resident((HEADS, CHUNK, HEAD_DIM))
    out = jax.ShapeDtypeStruct((batch, seq, D_MODEL), bf16)
    return pl.pallas_call(
        functools.partial(_mixer_kernel, chunk_decay),
        grid=(seq // CHUNK,),
        in_specs=[tile_spec, _resident(meta_pad.shape), _resident(meta_tm.shape),
                  _resident((1, D_MODEL)),
                  _wcols(COL_Q), _wcols(COL_K), _wcols(COL_V), _wcols(COL_GRET),
                  _wcols(COL_LIN),
                  tab_spec, tab_spec,
                  _resident((CHUNK, HEAD_DIM)), _resident((CHUNK, HEAD_DIM)),
                  const_spec, const_spec, const_spec]
                 + [_resident(a.shape) for a in lru_params],
        out_specs=[tile_spec, tile_spec],
        out_shape=[out, out],
        scratch_shapes=[pltpu.VMEM((batch, HEADS, HEAD_DIM, HEAD_DIM), f32),
                        pltpu.VMEM((HIST_STEPS * SUBLANES, D_MODEL), f32),
                        pltpu.VMEM((SUBLANES, D_MODEL), f32),
                        pltpu.VMEM((D_MODEL // LANES, batch * PITCH, LANES), f32)],
        compiler_params=pltpu.CompilerParams(
            dimension_semantics=("arbitrary",), vmem_limit_bytes=VMEM_LIMIT_BYTES),
        name="mixer",
    )(x, meta_pad, meta_tm, norm_w, w_in, w_in, w_in, w_in, w_in, *rot, *consts,
      *lru_params)


def _tail_kernel(x_ref, za_ref, hb_ref, n1_ref, wlg_ref, wga_ref, wgb_ref, wr_ref, wl_ref, wo_ref,
                 n2_ref, wfi_ref, wfo_ref, n3_ref, o_ref, act_ref):
    n_sub = x_ref.shape[0] // TAIL_SUB_ROWS
    subs = [slice(i * TAIL_SUB_ROWS, (i + 1) * TAIL_SUB_ROWS) for i in range(n_sub)]
    x = [x_ref[r, :] for r in subs]
    u1 = [_rmsnorm(xi, n1_ref[...]).astype(bf16) for xi in x]
    zb = [(_gelu_tanh(_dot(u1[i], wlg_ref[...])) * hb_ref[r, :].astype(f32)).astype(bf16)
          for i, r in enumerate(subs)]
    mixed = [jax.nn.sigmoid(_dot(u1[i], wga_ref[...])) * _dot(za_ref[r, :], wr_ref[...])
             + jax.nn.sigmoid(_dot(u1[i], wgb_ref[...])) * _dot(zb[i], wl_ref[...])
             for i, r in enumerate(subs)]
    h = [x[i] + _dot(mixed[i].astype(bf16), wo_ref[...]) for i in range(n_sub)]
    u = [_rmsnorm(hi, n2_ref[...]).astype(bf16) for hi in h]
    for j in range(FFN_HIDDEN // FFN_COLS):
        for i, r in enumerate(subs):
            g = _dot(u[i], wfi_ref[:, j * FFN_COLS:(j + 1) * FFN_COLS])
            up = _dot(u[i], wfi_ref[:, FFN_HIDDEN + j * FFN_COLS:FFN_HIDDEN + (j + 1) * FFN_COLS])
            act_ref[r, j * FFN_COLS:(j + 1) * FFN_COLS] = (
                g * jax.nn.sigmoid(g) * up).astype(bf16)
    for i, r in enumerate(subs):
        o_ref[r, :] = _rmsnorm(h[i] + _dot(act_ref[r, :], wfo_ref[...]), n3_ref[...])


def _tail(x2d, za, zb, norm1, w_in, weights):
    n = x2d.shape[0]
    row_spec = pl.BlockSpec((TAIL_ROWS, D_MODEL), lambda i: (i, 0))
    return pl.pallas_call(
        _tail_kernel,
        grid=(n // TAIL_ROWS,),
        in_specs=[row_spec, row_spec, row_spec, _resident((1, D_MODEL)),
                  _wcols(COL_LGATE), _wcols(COL_GA), _wcols(COL_GB)]
                 + [_resident(w.shape) for w in weights],
        out_specs=row_spec,
        out_shape=jax.ShapeDtypeStruct((n, D_MODEL), f32),
        scratch_shapes=[pltpu.VMEM((TAIL_ROWS, FFN_HIDDEN), bf16)],
        compiler_params=pltpu.CompilerParams(
            dimension_semantics=("parallel",), vmem_limit_bytes=VMEM_LIMIT_BYTES),
        name="merge_ffn",
    )(x2d, za, zb, norm1, w_in, w_in, w_in, *weights)


def kernel(x, meta_tokens, mix_norm_w, w_in, conv_w, conv_b, lru_wa, lru_ba, lru_wx, lru_bx,
           lru_lambda, w_branch_ret, w_branch_lru, w_out, ffn_norm_w, w_ffn_in, w_ffn_out,
           final_norm_w):
    batch, seq, d = x.shape
    assert d == D_MODEL and w_in.shape == (1, D_MODEL, 8 * D_MODEL), "single-layer block only"
    assert meta_tokens.shape == (N_META, D_MODEL)
    assert batch == SUBLANES and seq % CHUNK == 0 and (batch * seq) % TAIL_ROWS == 0

    row = lambda a: a.reshape(1, -1).astype(f32)
    meta = meta_tokens.astype(f32)
    meta_pad = jnp.concatenate([jnp.zeros((CHUNK - N_META, D_MODEL), f32), meta], axis=0)
    meta_tm = jnp.repeat(meta, SUBLANES, axis=0)
    w_in_b = w_in[0].astype(bf16)
    norm1 = row(mix_norm_w[0])

    wax = jnp.concatenate([lru_wa[0], lru_wx[0]], axis=-1).astype(bf16)
    bax = jnp.concatenate([lru_ba[0].reshape(LRU_BLOCKS, 1, LRU_BLOCK),
                           lru_bx[0].reshape(LRU_BLOCKS, 1, LRU_BLOCK)], axis=-1).astype(f32)
    lru_params = (conv_w[0].astype(f32), row(conv_b[0]), wax, bax, row(lru_lambda[0]))
    za, zb = _mixer(x, meta_pad, meta_tm, norm1, w_in_b, lru_params)

    weights = (w_branch_ret[0].astype(bf16), w_branch_lru[0].astype(bf16), w_out[0].astype(bf16),
               row(ffn_norm_w[0]), w_ffn_in[0].astype(bf16), w_ffn_out[0].astype(bf16),
               row(final_norm_w))
    n = batch * seq
    out = _tail(x.reshape(n, D_MODEL), za.reshape(n, D_MODEL), zb.reshape(n, D_MODEL),
                norm1, w_in_b, weights)
    return out.reshape(batch, seq, D_MODEL)
```

```python
import functools

import numpy as np
import jax
import jax.numpy as jnp
from jax import lax
from jax.experimental import pallas as pl
from jax.experimental.pallas import tpu as pltpu

D_MODEL = 1024
N_META = 16
HEADS = 8
HEAD_DIM = 128
CHUNK = 128
ROPE_BASE = 10000.0
LRU_BLOCKS = 4
LRU_BLOCK = D_MODEL // LRU_BLOCKS
CONV_WIDTH = 4
LRU_C = 8.0
FFN_HIDDEN = 2816
NORM_EPS = 1e-6

COL_Q, COL_K, COL_V, COL_GRET, COL_LIN, COL_LGATE, COL_GA, COL_GB = range(8)

SUBLANES = 8
LANES = 128
VMEM_LIMIT_BYTES = 60 * 1024 * 1024
TAIL_ROWS = 512
TAIL_SUB_ROWS = 256
FFN_COLS = 256
PITCH = CHUNK + SUBLANES
HIST_STEPS = CONV_WIDTH - 1
LRU_ROW_BLOCKS = 4
ROW_CHUNK = 64
PAIR = 2 * HEAD_DIM

f32 = jnp.float32
bf16 = jnp.bfloat16


def _rmsnorm(x, w=None):
    y = x * lax.rsqrt(jnp.mean(x * x, axis=-1, keepdims=True) + NORM_EPS)
    return y if w is None else y * w


def _dot(a, b):
    return jnp.dot(a, b, preferred_element_type=f32)


def _resident(shape):
    nd = len(shape)
    return pl.BlockSpec(shape, lambda *_: (0,) * nd, pipeline_mode=pl.Buffered(1))


def _wcols(col):
    return pl.BlockSpec((D_MODEL, D_MODEL), lambda *_: (0, col), pipeline_mode=pl.Buffered(1))


def _rot(x, cos, sin_signed):
    return x * cos + pltpu.roll(x, HEAD_DIM // 2, axis=1) * sin_signed


def _dot_t0(a, b):
    return lax.dot_general(a, b, (((0,), (0,)), ((), ())), preferred_element_type=f32)


def _dot_t1(a, b):
    return lax.dot_general(a, b, (((1,), (1,)), ((), ())), preferred_element_type=f32)


def _retention_tables(seq):
    inv_freq = ROPE_BASE ** (-np.arange(0, HEAD_DIM, 2, dtype=np.float64) / HEAD_DIM)

    def tables(pos):
        ang = pos.astype(np.float64)[:, None] * inv_freq[None, :]
        cos = np.cos(ang)
        sin = np.sin(ang)
        return (jnp.asarray(np.concatenate([cos, cos], axis=1), f32),
                jnp.asarray(np.concatenate([-sin, sin], axis=1), f32))

    cq, sq = tables(np.arange(seq) + N_META)
    cm, sm = tables(np.maximum(np.arange(CHUNK) - (CHUNK - N_META), 0))
    scale = HEAD_DIM ** -0.5
    log_g = np.log(1.0 - 2.0 ** (-5.0 - np.arange(HEADS, dtype=np.float64)))
    idx = np.arange(CHUNK, dtype=np.float64)
    diff = idx[:, None] - idx[None, :]
    intra = np.where(diff[None] >= 0, np.exp(np.maximum(diff, 0.0)[None] * log_g[:, None, None]), 0.0)
    ones = np.ones((1, 1, HEAD_DIM))
    qd = np.exp((idx + 1.0)[None, :, None] * log_g[:, None, None]) * ones
    kd = np.exp((CHUNK - 1.0 - idx)[None, :, None] * log_g[:, None, None]) * ones
    chunk_decay = tuple(float(v) for v in np.exp(CHUNK * log_g))
    consts = tuple(jnp.asarray(a, f32) for a in (intra * scale, qd, kd * scale))
    return (cq, sq, cm, sm), consts, chunk_decay


def _gelu_tanh(x):
    k0 = -2.0 * 0.7978845608028654
    k1 = -2.0 * 0.035677408136300125
    return x / (1.0 + jnp.exp(x * (k0 + k1 * (x * x))))


def _lru_gates(x_tm, prev, p, g):
    cw_ref, cb_ref, wax_ref, bax_ref, lam_ref = p
    cols = slice(g * LRU_BLOCK, (g + 1) * LRU_BLOCK)
    n = x_tm.shape[0]
    x_ext = jnp.concatenate([prev, x_tm], axis=0)
    chunks = [slice(r0, r0 + ROW_CHUNK) for r0 in range(0, n, ROW_CHUNK)]
    c = []
    for rows in chunks:
        cc = cb_ref[:, cols] + cw_ref[CONV_WIDTH - 1:CONV_WIDTH, cols] * x_tm[rows]
        for j in range(CONV_WIDTH - 1):
            cc = cc + cw_ref[j:j + 1, cols] * x_ext[j * SUBLANES + rows.start:
                                                    j * SUBLANES + rows.stop]
        c.append(cc)
    pre = _dot(jnp.concatenate(c, axis=0).astype(bf16), wax_ref[g])
    z = -lam_ref[:, cols]
    decay = LRU_C * (jnp.maximum(z, 0.0) + jnp.log1p(jnp.exp(-jnp.abs(z))))
    a_parts, u_parts = [], []
    for cc, rows in zip(c, chunks):
        ri = jax.nn.sigmoid(pre[rows] + bax_ref[g])
        r = ri[:, :LRU_BLOCK]
        i = ri[:, LRU_BLOCK:]
        neg_log_a = r * decay
        a = jnp.exp(-neg_log_a)
        y = jnp.tanh(neg_log_a) * (a * a + 1.0)
        mult = jnp.where(y > 0.0, y * lax.rsqrt(y), 0.0)
        a_parts.append(a)
        u_parts.append(mult * (i * cc))
    return jnp.concatenate(a_parts, axis=0), jnp.concatenate(u_parts, axis=0), x_ext[n:]


def _lru_scan(a, u, h):
    states = []
    for step in range(a.shape[0] // SUBLANES):
        rows = slice(step * SUBLANES, (step + 1) * SUBLANES)
        h = a[rows] * h + u[rows]
        states.append(h)
    return states


def _mixer_kernel(chunk_decay, x_ref, metap_ref, metat_ref, nw_ref,
                  wq_ref, wk_ref, wv_ref, wg_ref, wl_ref,
                  cq_ref, sq_ref, cm_ref, sm_ref, intra_ref, qd_ref, kd_ref,
                  cw_ref, cb_ref, wax_ref, bax_ref, lam_ref,
                  za_ref, zb_ref, state_ref, hist_ref, hstate_ref, stage_ref):
    batch = x_ref.shape[0]
    assert batch == SUBLANES
    t = pl.program_id(0)
    p = (cw_ref, cb_ref, wax_ref, bax_ref, lam_ref)
    slabs = LRU_BLOCK // LANES
    half = D_MODEL // 2

    @pl.when(t == 0)
    def _():
        um = _rmsnorm(metap_ref[...]).astype(bf16)
        km = _dot(um, wk_ref[...])
        vm = _dot(um, wv_ref[...]).astype(bf16)
        for h in range(HEADS):
            cols = slice(h * HEAD_DIM, (h + 1) * HEAD_DIM)
            kr = _rot(km[:, cols], cm_ref[...], sm_ref[...])
            st = _dot_t0((kr * kd_ref[h]).astype(bf16), vm[:, cols])
            for b in range(batch):
                state_ref[b, h] = st
        lin = _dot(_rmsnorm(metat_ref[...]).astype(bf16), wl_ref[...])
        for g in range(LRU_BLOCKS):
            cols = slice(g * LRU_BLOCK, (g + 1) * LRU_BLOCK)
            a, uin, hist = _lru_gates(lin[:, cols],
                                      jnp.zeros((HIST_STEPS * SUBLANES, LRU_BLOCK), f32), p, g)
            hist_ref[:, cols] = hist
            hstate_ref[:, cols] = _lru_scan(a, uin, jnp.zeros((SUBLANES, LRU_BLOCK), f32))[-1]

    u = _rmsnorm(x_ref[...].reshape(batch * CHUNK, D_MODEL)).astype(bf16)
    cq = cq_ref[...]
    sq = sq_ref[...]

    def zip_tasks(*task_lists):
        n = max(len(tl) for tl in task_lists)
        for k in range(n):
            for tl in task_lists:
                for task in tl[k * len(tl) // n:(k + 1) * len(tl) // n]:
                    task()

    def lru_project(b):
        lin = _dot(u[b * CHUNK:(b + 1) * CHUNK], wl_ref[...])
        for s in range(D_MODEL // LANES):
            stage_ref[s, b * PITCH:b * PITCH + CHUNK, :] = lin[:, s * LANES:(s + 1) * LANES]

    lru = {}
    steps_per_block = CHUNK // LRU_ROW_BLOCKS

    def lru_rows(g, rb):
        cols = slice(g * LRU_BLOCK, (g + 1) * LRU_BLOCK)
        steps = range(rb * steps_per_block, (rb + 1) * steps_per_block)
        x_tm = jnp.concatenate(
            [jnp.concatenate([stage_ref[g * slabs + s, pl.ds(step, SUBLANES, stride=PITCH), :]
                              for step in steps], axis=0) for s in range(slabs)], axis=1)
        prev = hist_ref[:, cols] if rb == 0 else lru[g, rb - 1][2]
        lru[g, rb] = _lru_gates(x_tm, prev, p, g)
        if rb == LRU_ROW_BLOCKS - 1:
            hist_ref[:, cols] = lru[g, rb][2]

    def lru_recurrence(g):
        cols = slice(g * LRU_BLOCK, (g + 1) * LRU_BLOCK)
        h = hstate_ref[:, cols]
        for rb in range(LRU_ROW_BLOCKS):
            a, uin, _ = lru[g, rb]
            for i, h in enumerate(_lru_scan(a, uin, h)):
                step = rb * steps_per_block + i
                for s in range(slabs):
                    stage_ref[g * slabs + s, pl.ds(step, SUBLANES, stride=PITCH), :] = (
                        h[:, s * LANES:(s + 1) * LANES])
        hstate_ref[:, cols] = h

    def lru_emit(c0, b):
        hb = jnp.concatenate([stage_ref[(c0 // LANES) + s, b * PITCH:b * PITCH + CHUNK, :]
                              for s in range(half // LANES)], axis=1)
        zb_ref[b, :, c0:c0 + half] = hb.astype(bf16)

    items = [(b, hh) for b in range(batch) for hh in range(2)]
    ret = {}

    def ret_project(hp, bp):
        rows = slice(bp * 2 * CHUNK, (bp + 1) * 2 * CHUNK)
        pcols = slice(hp * PAIR, (hp + 1) * PAIR)
        ret[hp, bp] = (_dot(u[rows], wq_ref[:, pcols]), _dot(u[rows], wk_ref[:, pcols]),
                       _dot(u[rows], wv_ref[:, pcols]).astype(bf16),
                       _dot(u[rows], wg_ref[:, pcols]))

    def sub(hp, which, b, hh):
        a = ret[hp, b // 2][which]
        return a[(b % 2) * CHUNK:(b % 2 + 1) * CHUNK, hh * HEAD_DIM:(hh + 1) * HEAD_DIM]

    def ret_scores(hp, it):
        qr = _rot(sub(hp, 0, *it), cq, sq)
        kr = _rot(sub(hp, 1, *it), cq, sq)
        ret[hp, it] = (qr, kr, _dot_t1(qr.astype(bf16), kr.astype(bf16)))

    def ret_outputs(hp, it):
        b, hh = it
        h = 2 * hp + hh
        qr, kr, s = ret[hp, it]
        v = sub(hp, 2, *it)
        state = state_ref[b, h]
        lhs = jnp.concatenate([(s * intra_ref[h]).astype(bf16),
                               (qr * qd_ref[h]).astype(bf16)], axis=1)
        rhs = jnp.concatenate([v, state.astype(bf16)], axis=0)
        ret[hp, it] = _dot(lhs, rhs)
        state_ref[b, h] = (state * chunk_decay[h]
                           + _dot_t0((kr * kd_ref[h]).astype(bf16), v))

    def ret_emit(hp, it):
        b, hh = it
        h = 2 * hp + hh
        o = ret[hp, it]
        on = o * lax.rsqrt(jnp.mean(o * o, axis=-1, keepdims=True) + NORM_EPS)
        g = sub(hp, 3, *it)
        za_ref[b, :, h * HEAD_DIM:(h + 1) * HEAD_DIM] = (
            g * jax.nn.sigmoid(g) * on).astype(bf16)

    def tasks(fn, *fixed, over):
        return [functools.partial(fn, *fixed, k) for k in over]

    zip_tasks(tasks(lru_project, over=range(batch)))
    for hp in range(HEADS // 2):
        zip_tasks(tasks(ret_project, hp, over=range(batch // 2)),
                  tasks(lru_rows, hp, over=range(LRU_ROW_BLOCKS)))
        emit = tasks(lru_emit, (hp // 2) * half, over=range(batch)) if hp % 2 == 1 else []
        for part in range(2):
            group = items[part * len(items) // 2:(part + 1) * len(items) // 2]
            zip_tasks(tasks(ret_scores, hp, over=group))
            if part == 0:
                lru_recurrence(hp)
            zip_tasks(tasks(ret_outputs, hp, over=group))
            zip_tasks(tasks(ret_emit, hp, over=group),
                      emit[part * len(emit) // 2:(part + 1) * len(emit) // 2])


def _mixer(x, meta_pad, meta_tm, norm_w, w_in, lru_params):
    batch, seq, _ = x.shape
    rot, consts, chunk_decay = _retention_tables(seq)
    tile_spec = pl.BlockSpec((batch, CHUNK, D_MODEL), lambda t: (0, t, 0))
    tab_spec = pl.BlockSpec((CHUNK, HEAD_DIM), lambda t: (t, 0))
    const_spec = _resident((HEADS, CHUNK, HEAD_DIM))
    out = jax.ShapeDtypeStruct((batch, seq, D_MODEL), bf16)
    return pl.pallas_call(
        functools.partial(_mixer_kernel, chunk_decay),
        grid=(seq // CHUNK,),
        in_specs=[tile_spec, _resident(meta_pad.shape), _resident(meta_tm.shape),
                  _resident((1, D_MODEL)),
                  _wcols(COL_Q), _wcols(COL_K), _wcols(COL_V), _wcols(COL_GRET),
                  _wcols(COL_LIN),
                  tab_spec, tab_spec,
                  _resident((CHUNK, HEAD_DIM)), _resident((CHUNK, HEAD_DIM)),
                  const_spec, const_spec, const_spec]
                 + [_resident(a.shape) for a in lru_params],
        out_specs=[tile_spec, tile_spec],
        out_shape=[out, out],
        scratch_shapes=[pltpu.VMEM((batch, HEADS, HEAD_DIM, HEAD_DIM), f32),
                        pltpu.VMEM((HIST_STEPS * SUBLANES, D_MODEL), f32),
                        pltpu.VMEM((SUBLANES, D_MODEL), f32),
                        pltpu.VMEM((D_MODEL // LANES, batch * PITCH, LANES), f32)],
        compiler_params=pltpu.CompilerParams(
            dimension_semantics=("arbitrary",), vmem_limit_bytes=VMEM_LIMIT_BYTES),
        name="mixer",
    )(x, meta_pad, meta_tm, norm_w, w_in, w_in, w_in, w_in, w_in, *rot, *consts,
      *lru_params)


def _tail_kernel(x_ref, za_ref, hb_ref, n1_ref, wlg_ref, wga_ref, wgb_ref, wr_ref, wl_ref, wo_ref,
                 n2_ref, wfi_ref, wfo_ref, n3_ref, o_ref, act_ref):
    n_sub = x_ref.shape[0] // TAIL_SUB_ROWS
    subs = [slice(i * TAIL_SUB_ROWS, (i + 1) * TAIL_SUB_ROWS) for i in range(n_sub)]
    x = [x_ref[r, :] for r in subs]
    u1 = [_rmsnorm(xi).astype(bf16) for xi in x]
    zb = [(_gelu_tanh(_dot(u1[i], wlg_ref[...])) * hb_ref[r, :].astype(f32)).astype(bf16)
          for i, r in enumerate(subs)]
    mixed = [jax.nn.sigmoid(_dot(u1[i], wga_ref[...])) * _dot(za_ref[r, :], wr_ref[...])
             + jax.nn.sigmoid(_dot(u1[i], wgb_ref[...])) * _dot(zb[i], wl_ref[...])
             for i, r in enumerate(subs)]
    h = [x[i] + _dot(mixed[i].astype(bf16), wo_ref[...]) for i in range(n_sub)]
    u = [_rmsnorm(hi).astype(bf16) for hi in h]
    for j in range(FFN_HIDDEN // FFN_COLS):
        for i, r in enumerate(subs):
            g = _dot(u[i], wfi_ref[:, j * FFN_COLS:(j + 1) * FFN_COLS])
            up = _dot(u[i], wfi_ref[:, FFN_HIDDEN + j * FFN_COLS:FFN_HIDDEN + (j + 1) * FFN_COLS])
            act_ref[r, j * FFN_COLS:(j + 1) * FFN_COLS] = (
                g * jax.nn.sigmoid(g) * up).astype(bf16)
    for i, r in enumerate(subs):
        o_ref[r, :] = _rmsnorm(h[i] + _dot(act_ref[r, :], wfo_ref[...]), n3_ref[...])


def _tail(x2d, za, zb, norm1, w_in, weights):
    n = x2d.shape[0]
    row_spec = pl.BlockSpec((TAIL_ROWS, D_MODEL), lambda i: (i, 0))
    return pl.pallas_call(
        _tail_kernel,
        grid=(n // TAIL_ROWS,),
        in_specs=[row_spec, row_spec, row_spec, _resident((1, D_MODEL)),
                  _wcols(COL_LGATE), _wcols(COL_GA), _wcols(COL_GB)]
                 + [_resident(w.shape) for w in weights],
        out_specs=row_spec,
        out_shape=jax.ShapeDtypeStruct((n, D_MODEL), f32),
        scratch_shapes=[pltpu.VMEM((TAIL_ROWS, FFN_HIDDEN), bf16)],
        compiler_params=pltpu.CompilerParams(
            dimension_semantics=("parallel",), vmem_limit_bytes=VMEM_LIMIT_BYTES),
        name="merge_ffn",
    )(x2d, za, zb, norm1, w_in, w_in, w_in, *weights)


def kernel(x, meta_tokens, mix_norm_w, w_in, conv_w, conv_b, lru_wa, lru_ba, lru_wx, lru_bx,
           lru_lambda, w_branch_ret, w_branch_lru, w_out, ffn_norm_w, w_ffn_in, w_ffn_out,
           final_norm_w):
    batch, seq, d = x.shape
    assert d == D_MODEL and w_in.shape == (1, D_MODEL, 8 * D_MODEL), "single-layer block only"
    assert meta_tokens.shape == (N_META, D_MODEL)
    assert batch == SUBLANES and seq % CHUNK == 0 and (batch * seq) % TAIL_ROWS == 0

    row = lambda a: a.reshape(1, -1).astype(f32)
    meta = meta_tokens.astype(f32)
    meta_pad = jnp.concatenate([jnp.zeros((CHUNK - N_META, D_MODEL), f32), meta], axis=0)
    meta_tm = jnp.repeat(meta, SUBLANES, axis=0)
    w_in_b = (w_in[0] * mix_norm_w[0].astype(f32)[:, None]).astype(bf16)
    norm1 = row(mix_norm_w[0])

    wax = jnp.concatenate([lru_wa[0], lru_wx[0]], axis=-1).astype(bf16)
    bax = jnp.concatenate([lru_ba[0].reshape(LRU_BLOCKS, 1, LRU_BLOCK),
                           lru_bx[0].reshape(LRU_BLOCKS, 1, LRU_BLOCK)], axis=-1).astype(f32)
    lru_params = (conv_w[0].astype(f32), row(conv_b[0]), wax, bax, row(lru_lambda[0]))
    za, zb = _mixer(x, meta_pad, meta_tm, norm1, w_in_b, lru_params)

    weights = (w_branch_ret[0].astype(bf16), w_branch_lru[0].astype(bf16), w_out[0].astype(bf16),
               row(ffn_norm_w[0]),
               (w_ffn_in[0] * ffn_norm_w[0].astype(f32)[:, None]).astype(bf16),
               w_ffn_out[0].astype(bf16),
               row(final_norm_w))
    n = batch * seq
    out = _tail(x.reshape(n, D_MODEL), za.reshape(n, D_MODEL), zb.reshape(n, D_MODEL),
                norm1, w_in_b, weights)
    return out.reshape(batch, seq, D_MODEL)
```
